```python
import math
import jax, jax.numpy as jnp
from jax import lax
import numpy as np

D_MODEL = 1024
BATCH = 4
SEQ = 8192
DEPTH = 1

GRID_W = 64
CTX_LEN = 256

D_INNER = 2 * D_MODEL
HEAD_DIM = 64
N_HEADS = D_INNER // HEAD_DIM
N_GROUPS = 8
HPG = N_HEADS // N_GROUPS
D_STATE = 128
SSM_CONV = 4
SSM_PAD = (2, 1)
CHUNK = 128

D_CONF = D_MODEL
CONF_KERNEL = 31
CONF_PAD = (CONF_KERNEL // 2, CONF_KERNEL // 2)

EPS = 1e-6

GN = N_GROUPS * D_STATE
X_END = D_INNER
B_END = X_END + GN
C_END = B_END + GN
DT_END = C_END + 2 * N_HEADS
Z_END = DT_END + D_INNER
GLU_END = Z_END + 2 * D_CONF
CG_END = GLU_END + D_CONF
IN_COLS = CG_END + 2 * D_MODEL

kernel_name = 'hybrid_ssd_conformer_prefix_block'


def rms_norm(x, w):
    xf = x.astype(jnp.float32)
    y = xf * lax.rsqrt(jnp.mean(xf * xf, axis=-1, keepdims=True) + EPS)
    return y.astype(x.dtype) * w


def group_rms_norm(x, w):
    shp = x.shape
    xg = x.reshape(*shp[:-1], N_GROUPS, shp[-1] // N_GROUPS).astype(jnp.float32)
    y = xg * lax.rsqrt(jnp.mean(xg * xg, axis=-1, keepdims=True) + EPS)
    return y.reshape(shp).astype(x.dtype) * w


def layer_norm(x, w, b):
    xf = x.astype(jnp.float32)
    mu = jnp.mean(xf, axis=-1, keepdims=True)
    var = jnp.mean(jnp.square(xf - mu), axis=-1, keepdims=True)
    return ((xf - mu) * lax.rsqrt(var + EPS)).astype(x.dtype) * w + b


def modulate(h, shift, scale):
    return h * (1 + scale) + shift


def depthwise_conv(u, w, b, pad):
    out = lax.conv_general_dilated(u, w[:, None, :], (1,), [pad],
                                   dimension_numbers=('NWC', 'WIO', 'NWC'),
                                   feature_group_count=u.shape[-1])
    return out + b


def rev(t):
    return jnp.flip(t, axis=1)


def to_chunks(t):
    return t.reshape(t.shape[0], t.shape[1] // CHUNK, CHUNK, *t.shape[2:])


def ssm_dt(dt_raw, dt_bias):
    b, L = dt_raw.shape[:2]
    return jax.nn.softplus(dt_raw.astype(jnp.float32).reshape(b, L, 2, N_GROUPS, HPG)
                           + dt_bias.astype(jnp.float32).reshape(2, N_GROUPS, HPG))


def ssm_decay(a_log):
    return -jnp.exp(a_log.astype(jnp.float32)).reshape(2, N_GROUPS, HPG)


def chunk_states(xs, dt, a, bm, h0):
    la = jnp.cumsum(dt * a, axis=2)
    w_end = jnp.exp(la[:, :, -1:] - la) * dt
    contrib = jnp.einsum('bcsgn,bcsgrp->bcgrpn', bm, xs * w_end[..., None])
    chunk_decay = jnp.exp(la[:, :, -1])

    def step(h, inp):
        s, d = inp
        return h * d[..., None, None] + s, h

    h_last, h_prev = lax.scan(step, h0, (jnp.moveaxis(contrib, 1, 0), jnp.moveaxis(chunk_decay, 1, 0)))
    return la, jnp.moveaxis(h_prev, 0, 1), h_last


def ssd_scan(xs, dt, a, bm, cm, h0):
    xs_c, dt_c, bm_c, cm_c = [to_chunks(t.astype(jnp.float32)) for t in (xs, dt, bm, cm)]
    la, h_prev, h_last = chunk_states(xs_c, dt_c, a, bm_c, h0)
    idx = jnp.arange(CHUNK)
    order = (idx[:, None] >= idx[None, :])[None, None, :, :, None, None]
    seg = la[:, :, :, None] - la[:, :, None, :]
    decay = jnp.exp(jnp.where(order, seg, -jnp.inf))
    scores = jnp.einsum('bclgn,bcsgn->bclsg', cm_c, bm_c)
    mix = scores[..., None] * decay * dt_c[:, :, None]
    y_diag = jnp.einsum('bclsgr,bcsgrp->bclgrp', mix, xs_c)
    y_off = jnp.einsum('bclgn,bcgrpn->bclgrp', cm_c, h_prev) * jnp.exp(la)[..., None]
    return (y_diag + y_off).reshape(xs.shape), h_last


def ssd_final_state(xs, dt, a, bm, h0):
    xs_c, dt_c, bm_c = [to_chunks(t.astype(jnp.float32)) for t in (xs, dt, bm)]
    _, _, h_last = chunk_states(xs_c, dt_c, a, bm_c, h0)
    return h_last


def context_states(h_ctx, w_in, ssm_conv_w, ssm_conv_b, dt_bias, a_log, h0):
    b, L, _ = h_ctx.shape
    xb = jax.nn.silu(depthwise_conv(h_ctx @ w_in[:, :B_END], ssm_conv_w[:, :B_END], ssm_conv_b[:B_END], SSM_PAD))
    xs = xb[..., :X_END].reshape(b, L, N_GROUPS, HPG, HEAD_DIM)
    bm = xb[..., X_END:B_END].reshape(b, L, N_GROUPS, D_STATE)
    dt = ssm_dt(h_ctx @ w_in[:, C_END:DT_END], dt_bias)
    a = ssm_decay(a_log)
    h_f = ssd_final_state(xs, dt[:, :, 0], a[0], bm, h0)
    h_b = ssd_final_state(rev(xs), rev(dt[:, :, 1]), a[1], rev(bm), h0)
    return h_f, h_b


def mixer(h, p, h0_f, h0_b, rows):
    b, L, _ = h.shape
    proj = h @ p['w_in']
    xbc = jax.nn.silu(depthwise_conv(proj[..., :C_END], p['ssm_conv_w'], p['ssm_conv_b'], SSM_PAD))
    xs = xbc[..., :X_END].reshape(b, L, N_GROUPS, HPG, HEAD_DIM)
    bm = xbc[..., X_END:B_END].reshape(b, L, N_GROUPS, D_STATE)
    cm = xbc[..., B_END:C_END].reshape(b, L, N_GROUPS, D_STATE)
    dt = ssm_dt(proj[..., C_END:DT_END], p['dt_bias'])
    a = ssm_decay(p['a_log'])
    y_f, h_f = ssd_scan(xs, dt[:, :, 0], a[0], bm, cm, h0_f)
    y_b, h_b = ssd_scan(rev(xs), rev(dt[:, :, 1]), a[1], rev(bm), rev(cm), h0_b)
    y = (y_f + rev(y_b)).astype(h.dtype) + p['d_skip'].reshape(N_GROUPS, HPG, 1) * xs
    y = y.reshape(b, L, D_INNER) * jax.nn.silu(proj[..., DT_END:Z_END])
    branch_ssm = group_rms_norm(y, p['ssm_norm_w']) @ p['w_out_ssm']
    glu = proj[..., Z_END:GLU_END]
    u = glu[..., :D_CONF] * jax.nn.sigmoid(glu[..., D_CONF:])
    if rows is not None:
        u = u.reshape(b * rows, GRID_W, D_CONF)
    u = depthwise_conv(u, p['conf_conv_w'], p['conf_conv_b'], CONF_PAD).reshape(b, L, D_CONF)
    u = jax.nn.silu(layer_norm(u, p['conf_ln_w'], p['conf_ln_b'])) * jax.nn.silu(proj[..., GLU_END:CG_END])
    branch_conf = u @ p['w_out_conf']
    g = jax.nn.sigmoid(proj[..., CG_END:])
    merged = g[..., :D_MODEL] * branch_ssm + g[..., D_MODEL:] * branch_conf
    return merged @ p['w_out'], h_f, h_b


def setup_inputs(seed: int = 0) -> dict:
    key = jax.random.key(seed)
    ks = jax.random.split(key, 24)
    f32 = jnp.float32

    def nrm(k, shape, s):
        return jax.random.normal(k, shape, f32) * s

    dt0 = jnp.exp(jax.random.uniform(ks[10], (DEPTH, 2, N_HEADS), f32, math.log(1e-3), math.log(1e-1)))
    return {
        'x': nrm(ks[0], (BATCH, SEQ, D_MODEL), 1.0),
        'c': nrm(ks[1], (BATCH, D_MODEL), 1.0),
        'ctx': nrm(ks[2], (BATCH, CTX_LEN, D_MODEL), 1.0),
        'c_ctx': nrm(ks[3], (D_MODEL,), 1.0),
        'w_mod': nrm(ks[4], (DEPTH, D_MODEL, 3 * D_MODEL), D_MODEL ** -0.5),
        'b_mod': nrm(ks[5], (DEPTH, 3 * D_MODEL), 0.01),
        'norm_w': 1.0 + nrm(ks[6], (DEPTH, D_MODEL), 0.01),
        'w_in': nrm(ks[7], (DEPTH, D_MODEL, IN_COLS), D_MODEL ** -0.5),
        'ssm_conv_w': nrm(ks[8], (DEPTH, SSM_CONV, C_END), SSM_CONV ** -0.5),
        'ssm_conv_b': nrm(ks[9], (DEPTH, C_END), 0.01),
        'dt_bias': dt0 + jnp.log(-jnp.expm1(-dt0)),
        'a_log': jnp.log(jax.random.uniform(ks[11], (DEPTH, 2, N_HEADS), f32, 1.0, 16.0)),
        'd_skip': 1.0 + nrm(ks[12], (DEPTH, N_HEADS), 0.01),
        'ssm_norm_w': 1.0 + nrm(ks[13], (DEPTH, D_INNER), 0.01),
        'w_out_ssm': nrm(ks[14], (DEPTH, D_INNER, D_MODEL), D_INNER ** -0.5),
        'conf_conv_w': nrm(ks[15], (DEPTH, CONF_KERNEL, D_CONF), CONF_KERNEL ** -0.5),
        'conf_conv_b': nrm(ks[16], (DEPTH, D_CONF), 0.01),
        'conf_ln_w': 1.0 + nrm(ks[17], (DEPTH, D_CONF), 0.01),
        'conf_ln_b': nrm(ks[18], (DEPTH, D_CONF), 0.01),
        'w_out_conf': nrm(ks[19], (DEPTH, D_CONF, D_MODEL), D_CONF ** -0.5),
        'w_out': nrm(ks[20], (DEPTH, D_MODEL, D_MODEL), D_MODEL ** -0.5),
        'final_norm_w': 1.0 + nrm(ks[21], (D_MODEL,), 0.01),
    }


def reference(x, c, ctx, c_ctx, w_mod, b_mod, norm_w, w_in, ssm_conv_w, ssm_conv_b, dt_bias, a_log,
              d_skip, ssm_norm_w, w_out_ssm, conf_conv_w, conf_conv_b, conf_ln_w, conf_ln_b,
              w_out_conf, w_out, final_norm_w):
    rows = x.shape[1] // GRID_W
    h0 = jnp.zeros((ctx.shape[0], N_GROUPS, HPG, HEAD_DIM, D_STATE), jnp.float32)
    for i in range(DEPTH):
        p = {'w_in': w_in[i], 'ssm_conv_w': ssm_conv_w[i], 'ssm_conv_b': ssm_conv_b[i],
             'dt_bias': dt_bias[i], 'a_log': a_log[i], 'd_skip': d_skip[i], 'ssm_norm_w': ssm_norm_w[i],
             'w_out_ssm': w_out_ssm[i], 'conf_conv_w': conf_conv_w[i], 'conf_conv_b': conf_conv_b[i],
             'conf_ln_w': conf_ln_w[i], 'conf_ln_b': conf_ln_b[i], 'w_out_conf': w_out_conf[i],
             'w_out': w_out[i]}
        mod_x = jax.nn.silu(c) @ w_mod[i] + b_mod[i]
        mod_c = jax.nn.silu(c_ctx) @ w_mod[i] + b_mod[i]
        shift_x, scale_x, gate_x = jnp.split(mod_x[:, None, :], 3, axis=-1)
        shift_c, scale_c, gate_c = jnp.split(mod_c, 3)
        h_ctx = modulate(rms_norm(ctx, norm_w[i]), shift_c, scale_c)
        if i < DEPTH - 1:
            ctx_out, h_f, h_b = mixer(h_ctx, p, h0, h0, None)
        else:
            h_f, h_b = context_states(h_ctx, p['w_in'], p['ssm_conv_w'], p['ssm_conv_b'],
                                      p['dt_bias'], p['a_log'], h0)
        h = modulate(rms_norm(x, norm_w[i]), shift_x, scale_x)
        out, _, _ = mixer(h, p, h_f, h_b, rows)
        x = x + gate_x * out
        if i < DEPTH - 1:
            ctx = ctx + gate_c * ctx_out
    return rms_norm(x, final_norm_w)
```

```python
import functools

import numpy as np
import jax
import jax.numpy as jnp
from jax import lax
from jax.experimental import pallas as pl
from jax.experimental.pallas import tpu as pltpu

F32 = jnp.float32
BF16 = jnp.bfloat16

D_MODEL = 1024
GRID_W = 64
D_INNER = 2 * D_MODEL
HEAD_DIM = 64
N_HEADS = D_INNER // HEAD_DIM
N_GROUPS = 8
HPG = N_HEADS // N_GROUPS
D_STATE = 128
SSM_CONV = 4
CHUNK = 128
D_CONF = D_MODEL
CONF_KERNEL = 31
EPS = 1e-6

GN = N_GROUPS * D_STATE
X_END = D_INNER
B_END = X_END + GN
C_END = B_END + GN
DT_END = C_END + 2 * N_HEADS
Z_END = DT_END + D_INNER
GLU_END = Z_END + 2 * D_CONF
CG_END = GLU_END + D_CONF
IN_COLS = CG_END + 2 * D_MODEL

GROUP_W = HPG * HEAD_DIM
DT_LANES = 128
P_XBC, P_Z, P_GLU, P_CG, P_GATE = 0, C_END, C_END + D_INNER, C_END + D_INNER + 2 * D_CONF, C_END + D_INNER + 3 * D_CONF
P_COLS = IN_COLS - 2 * N_HEADS

HALO = 16
CONF_PAD_ROWS = 16
CONF_SLOT = GRID_W + 2 * CONF_PAD_ROWS
VMEM_LIMIT_BYTES = 56 * 1024 * 1024


def _sigmoid(v):
    return 1.0 / (1.0 + jnp.exp(-v))


def _silu(v):
    return v * _sigmoid(v)


def _mod_kernel(c_ref, w_ref, b_ref, o_ref):
    s = _silu(c_ref[...])
    o_ref[...] = jnp.dot(s, w_ref[...], preferred_element_type=F32,
                         precision=lax.Precision.HIGHEST) + b_ref[...]


def _mod(c_all, w_mod, b_mod):
    rows, d = c_all.shape
    n = w_mod.shape[1]
    tn = 1024
    return pl.pallas_call(
        _mod_kernel,
        grid=(n // tn,),
        in_specs=[pl.BlockSpec((rows, d), lambda j: (0, 0)),
                  pl.BlockSpec((d, tn), lambda j: (0, j)),
                  pl.BlockSpec((1, tn), lambda j: (0, j))],
        out_specs=pl.BlockSpec((rows, tn), lambda j: (0, j)),
        out_shape=jax.ShapeDtypeStruct((rows, n), F32),
        name="mod",
    )(c_all, w_mod, b_mod.reshape(1, n))


def _inproj_kernel(x_ref, mod_ref, nw_ref, w_ref, wdt_ref, o_ref, dt_ref, h_ref):
    @pl.when(pl.program_id(1) == 0)
    def _():
        x = x_ref[...]
        ms = jnp.mean(x * x, axis=-1, keepdims=True)
        y = x * lax.rsqrt(ms + EPS) * nw_ref[...]
        h = y * (1.0 + mod_ref[0, 1:2, :]) + mod_ref[0, 0:1, :]
        hb = h.astype(BF16)
        h_ref[...] = hb
        dt_ref[...] = jnp.dot(hb, wdt_ref[...], preferred_element_type=F32)

    o_ref[...] = jnp.dot(h_ref[...], w_ref[...], preferred_element_type=F32).astype(BF16)


def _in_proj(x2d, mod3, norm_w, w_main, w_dt, *, tm, tn, n_col_tiles, mod_row):
    m, d = x2d.shape
    return pl.pallas_call(
        _inproj_kernel,
        grid=(m // tm, n_col_tiles),
        in_specs=[pl.BlockSpec((tm, d), lambda i, j: (i, 0)),
                  pl.BlockSpec((1, 3, d), lambda i, j: (mod_row(i), 0, 0)),
                  pl.BlockSpec((1, d), lambda i, j: (0, 0)),
                  pl.BlockSpec((d, tn), lambda i, j: (0, j)),
                  pl.BlockSpec((d, DT_LANES), lambda i, j: (0, 0))],
        out_specs=[pl.BlockSpec((tm, tn), lambda i, j: (i, j)),
                   pl.BlockSpec((tm, DT_LANES), lambda i, j: (i, 0))],
        out_shape=[jax.ShapeDtypeStruct((m, n_col_tiles * tn), BF16),
                   jax.ShapeDtypeStruct((m, DT_LANES), F32)],
        scratch_shapes=[pltpu.VMEM((tm, d), BF16)],
        compiler_params=pltpu.CompilerParams(
            dimension_semantics=("parallel", "arbitrary"), vmem_limit_bytes=VMEM_LIMIT_BYTES),
        name="in_proj",
    )(x2d, mod3, norm_w, w_main, w_dt)


def _dt_kernel(raw_ref, bias_ref, alog_ref, o_ref):
    v = raw_ref[...] + bias_ref[...]
    dt = jnp.maximum(v, 0.0) + jnp.log1p(jnp.exp(-jnp.abs(v)))
    dta = dt * (-jnp.exp(alog_ref[...]))
    row = lax.broadcasted_iota(jnp.int32, (CHUNK, CHUNK), 0)
    col = lax.broadcasted_iota(jnp.int32, (CHUNK, CHUNK), 1)
    tri_lo = (col <= row).astype(F32)
    tri_up = (col >= row).astype(F32)
    la_f = jnp.dot(tri_lo, dta, preferred_element_type=F32, precision=lax.Precision.HIGHEST)
    la_b = jnp.dot(tri_up, dta, preferred_element_type=F32, precision=lax.Precision.HIGHEST)
    lane = lax.broadcasted_iota(jnp.int32, (1, DT_LANES), 1)
    is_fwd = (lane % (2 * HPG)) < HPG
    la = jnp.where(is_fwd, la_f, la_b)
    la_end = jnp.where(is_fwd, la[CHUNK - 1:CHUNK, :], la[0:1, :])
    ela = jnp.exp(la)
    wend = jnp.exp(la_end - la) * dt
    for q, val in enumerate((la, dt, ela, wend)):
        o_ref[0, q] = val.T[:2 * N_HEADS, :]


def _dt_prep(dt_raw, bias_p, alog_p, batch, seq):
    nc = seq // CHUNK
    return pl.pallas_call(
        _dt_kernel,
        grid=(batch, nc),
        in_specs=[pl.BlockSpec((CHUNK, DT_LANES), lambda b, c: (b * nc + c, 0)),
                  pl.BlockSpec((1, DT_LANES), lambda b, c: (0, 0)),
                  pl.BlockSpec((1, DT_LANES), lambda b, c: (0, 0))],
        out_specs=pl.BlockSpec((1, 4, 2 * N_HEADS, CHUNK), lambda b, c: (b, 0, 0, c)),
        out_shape=jax.ShapeDtypeStruct((batch, 4, 2 * N_HEADS, seq), F32),
        compiler_params=pltpu.CompilerParams(dimension_semantics=("parallel", "parallel")),
        name="dt_prep",
    )(dt_raw, bias_p, alog_p)


def _chunk_start(c):
    return c * CHUNK if isinstance(c, int) else pl.multiple_of(c * CHUNK, CHUNK)


def _conv_silu_chunk(ref, c, nc, w, b):
    seq = ref.shape[0]
    r0 = _chunk_start(c)
    main = ref[pl.ds(r0, CHUNK), :].astype(F32)
    if isinstance(c, int):
        p0, n0 = max(r0 - HALO, 0), min(r0 + CHUNK, seq - HALO)
    else:
        p0 = pl.multiple_of(jnp.maximum(r0 - HALO, 0), HALO)
        n0 = pl.multiple_of(jnp.minimum(r0 + CHUNK, seq - HALO), HALO)
    prev = ref[pl.ds(p0, HALO), :].astype(F32)
    nxt = ref[pl.ds(n0, HALO), :].astype(F32)
    prev = jnp.where(c > 0, prev, 0.0)
    nxt = jnp.where(c < nc - 1, nxt, 0.0)
    ext = jnp.concatenate([prev, main, nxt], axis=0)
    rows = CHUNK + 2 * HALO
    xm2 = pltpu.roll(ext, 2, 0)[HALO:HALO + CHUNK]
    xm1 = pltpu.roll(ext, 1, 0)[HALO:HALO + CHUNK]
    xp1 = pltpu.roll(ext, rows - 1, 0)[HALO:HALO + CHUNK]
    y = w[0:1] * xm2 + w[1:2] * xm1 + w[2:3] * main + w[3:4] * xp1 + b
    return _silu(y)


def _col_forms(rp_ref, c):
    r0 = _chunk_start(c)
    parts = [rp_ref[0, q, :, pl.ds(r0, CHUNK)] for q in range(4)]
    pad = jnp.zeros((CHUNK - 4 * 2 * HPG, CHUNK), F32)
    return jnp.concatenate(parts + [pad], axis=0).T


def _expand_heads(col, base):
    lane = lax.broadcasted_iota(jnp.int32, (CHUNK, CHUNK), 1)
    lo = lane < HEAD_DIM
    bc = [jnp.broadcast_to(col[:, base + r:base + r + 1], (CHUNK, CHUNK)) for r in range(HPG)]
    return jnp.concatenate([jnp.where(lo, bc[0], bc[1]), jnp.where(lo, bc[2], bc[3])], axis=1)


Q_LA, Q_DT, Q_ELA, Q_WEND = 0, 1, 2, 3
LANES_PER_Q = 2 * HPG


def _state_contrib(bt_bf, xs, col, direction):
    wexp = _expand_heads(col, Q_WEND * LANES_PER_Q + direction * HPG)
    return jnp.dot(bt_bf, (xs * wexp).astype(BF16), preferred_element_type=F32)


def _ctx_kernel(xs_ref, b_ref, rp_ref, cwx_ref, cbx_ref, cwb_ref, cbb_ref, o_ref):
    nc = xs_ref.shape[0] // CHUNK
    cwx, cbx, cwb, cbb = cwx_ref[...], cbx_ref[...], cwb_ref[...], cbb_ref[...]
    xs, bt, col = [], [], []
    for c in range(nc):
        xs.append(_conv_silu_chunk(xs_ref, c, nc, cwx, cbx))
        bt.append(_conv_silu_chunk(b_ref, c, nc, cwb, cbb).T.astype(BF16))
        col.append(_col_forms(rp_ref, c))
    hf = jnp.zeros((D_STATE, GROUP_W), F32)
    for c in range(nc):
        dec = _expand_heads(col[c], Q_ELA * LANES_PER_Q)[CHUNK - 1:CHUNK, :]
        hf = hf * dec + _state_contrib(bt[c], xs[c], col[c], 0)
    hb = jnp.zeros((D_STATE, GROUP_W), F32)
    for c in reversed(range(nc)):
        dec = _expand_heads(col[c], Q_ELA * LANES_PER_Q + HPG)[0:1, :]
        hb = hb * dec + _state_contrib(bt[c], xs[c], col[c], 1)
    o_ref[0, 0, 0] = hf
    o_ref[0, 0, 1] = hb


def _ctx_states(proj_ctx, rowpack, conv_w, conv_b, batch, seq):
    xcol = X_END // GROUP_W
    bcol = X_END // D_STATE
    return pl.pallas_call(
        _ctx_kernel,
        grid=(batch, N_GROUPS),
        in_specs=[pl.BlockSpec((seq, GROUP_W), lambda b, g: (b, g)),
                  pl.BlockSpec((seq, D_STATE), lambda b, g: (b, bcol + g)),
                  pl.BlockSpec((1, 4, 2 * HPG, seq), lambda b, g: (b, 0, g, 0)),
                  pl.BlockSpec((SSM_CONV, GROUP_W), lambda b, g: (0, g)),
                  pl.BlockSpec((1, GROUP_W), lambda b, g: (0, g)),
                  pl.BlockSpec((SSM_CONV, D_STATE), lambda b, g: (0, bcol + g)),
                  pl.BlockSpec((1, D_STATE), lambda b, g: (0, bcol + g))],
        out_specs=pl.BlockSpec((1, 1, 2, D_STATE, GROUP_W), lambda b, g: (b, g, 0, 0, 0)),
        out_shape=jax.ShapeDtypeStruct((batch, N_GROUPS, 2, D_STATE, GROUP_W), F32),
        compiler_params=pltpu.CompilerParams(dimension_semantics=("parallel", "parallel")),
        name="ctx_states",
    )(proj_ctx, proj_ctx, rowpack, conv_w, conv_b, conv_w, conv_b)


def _ssd_kernel(xs_ref, b_ref, c_ref, z_ref, rp_ref, cwx_ref, cbx_ref, cwb_ref, cbb_ref,
                cwc_ref, cbc_ref, dsk_ref, nw_ref, h0_ref, o_ref,
                cx_s, cbt_s, cc_s, stash_s, hf_s, hb_s):
    seq = xs_ref.shape[0]
    nc = seq // CHUNK
    cwx, cbx = cwx_ref[...], cbx_ref[...]
    cwb, cbb = cwb_ref[...], cbb_ref[...]
    cwc, cbc = cwc_ref[...], cbc_ref[...]
    hf_s[...] = h0_ref[0, 0, 0]
    hb_s[...] = h0_ref[0, 0, 1]

    def back_body(i, carry):
        c = nc - 1 - i
        r0 = _chunk_start(c)
        xs = _conv_silu_chunk(xs_ref, c, nc, cwx, cbx)
        bt = _conv_silu_chunk(b_ref, c, nc, cwb, cbb).T.astype(BF16)
        cc = _conv_silu_chunk(c_ref, c, nc, cwc, cbc)
        cx_s[pl.ds(r0, CHUNK), :] = xs.astype(BF16)
        cbt_s[pl.ds(r0, CHUNK), :] = bt
        cc_s[pl.ds(r0, CHUNK), :] = cc.astype(BF16)
        col = _col_forms(rp_ref, c)
        hb = hb_s[...]
        stash_s[pl.ds(r0, CHUNK), :] = hb.astype(BF16)
        dec = _expand_heads(col, Q_ELA * LANES_PER_Q + HPG)[0:1, :]
        hb_s[...] = hb * dec + _state_contrib(bt, xs, col, 1)
        return carry

    lax.fori_loop(0, nc, back_body, 0)

    row = lax.broadcasted_iota(jnp.int32, (CHUNK, CHUNK), 0)
    lane = lax.broadcasted_iota(jnp.int32, (CHUNK, CHUNK), 1)
    causal = lane <= row
    anticausal = lane >= row
    lane_w = lax.broadcasted_iota(jnp.int32, (CHUNK, GROUP_W), 1)
    dsk, nw = dsk_ref[...], nw_ref[...]

    def fwd_body(c, carry):
        r0 = _chunk_start(c)
        xs_bf = cx_s[pl.ds(r0, CHUNK), :]
        xs = xs_bf.astype(F32)
        bt = cbt_s[pl.ds(r0, CHUNK), :]
        cc = cc_s[pl.ds(r0, CHUNK), :]
        col = _col_forms(rp_ref, c)
        scores = jnp.dot(cc, bt, preferred_element_type=F32)
        y = dsk * xs
        for r in range(HPG):
            la_f = rp_ref[0, Q_LA, r:r + 1, pl.ds(r0, CHUNK)]
            la_b = rp_ref[0, Q_LA, HPG + r:HPG + r + 1, pl.ds(r0, CHUNK)]
            dt_f = rp_ref[0, Q_DT, r:r + 1, pl.ds(r0, CHUNK)]
            dt_b = rp_ref[0, Q_DT, HPG + r:HPG + r + 1, pl.ds(r0, CHUNK)]
            seg_f = col[:, r:r + 1] - la_f
            seg_b = col[:, HPG + r:HPG + r + 1] - la_b
            m_f = jnp.exp(jnp.where(causal, seg_f, -jnp.inf)) * dt_f
            m_b = jnp.exp(jnp.where(anticausal, seg_b, -jnp.inf)) * dt_b
            mix = (scores * (m_f + m_b)).astype(BF16)
            head = (lane_w >= r * HEAD_DIM) & (lane_w < (r + 1) * HEAD_DIM)
            xs_r = jnp.where(head, xs_bf, jnp.zeros_like(xs_bf))
            y = y + jnp.dot(mix, xs_r, preferred_element_type=F32)
        hf = hf_s[...]
        e_f = _expand_heads(col, Q_ELA * LANES_PER_Q)
        e_b = _expand_heads(col, Q_ELA * LANES_PER_Q + HPG)
        y = y + jnp.dot(cc, hf.astype(BF16), preferred_element_type=F32) * e_f
        y = y + jnp.dot(cc, stash_s[pl.ds(r0, CHUNK), :], preferred_element_type=F32) * e_b
        hf_s[...] = hf * e_f[CHUNK - 1:CHUNK, :] + _state_contrib(bt, xs, col, 0)
        y = y * _silu(z_ref[pl.ds(r0, CHUNK), :].astype(F32))
        ms = jnp.mean(y * y, axis=-1, keepdims=True)
        o_ref[pl.ds(r0, CHUNK), :] = (y * lax.rsqrt(ms + EPS) * nw).astype(o_ref.dtype)
        return carry

    lax.fori_loop(0, nc, fwd_body, 0)


def _ssd(proj, rowpack, conv_w, conv_b, dskip_e, norm_w, h0, batch, seq):
    bcol = X_END // D_STATE
    ccol = B_END // D_STATE
    zcol = P_Z // GROUP_W
    gmap = lambda b, g: (0, g)
    return pl.pallas_call(
        _ssd_kernel,
        grid=(batch, N_GROUPS),
        in_specs=[pl.BlockSpec((seq, GROUP_W), lambda b, g: (b, g)),
                  pl.BlockSpec((seq, D_STATE), lambda b, g: (b, bcol + g)),
                  pl.BlockSpec((seq, D_STATE), lambda b, g: (b, ccol + g)),
                  pl.BlockSpec((seq, GROUP_W), lambda b, g: (b, zcol + g)),
                  pl.BlockSpec((1, 4, 2 * HPG, seq), lambda b, g: (b, 0, g, 0)),
                  pl.BlockSpec((SSM_CONV, GROUP_W), gmap),
                  pl.BlockSpec((1, GROUP_W), gmap),
                  pl.BlockSpec((SSM_CONV, D_STATE), lambda b, g: (0, bcol + g)),
                  pl.BlockSpec((1, D_STATE), lambda b, g: (0, bcol + g)),
                  pl.BlockSpec((SSM_CONV, D_STATE), lambda b, g: (0, ccol + g)),
                  pl.BlockSpec((1, D_STATE), lambda b, g: (0, ccol + g)),
                  pl.BlockSpec((1, GROUP_W), gmap),
                  pl.BlockSpec((1, GROUP_W), gmap),
                  pl.BlockSpec((1, 1, 2, D_STATE, GROUP_W), lambda b, g: (b, g, 0, 0, 0))],
        out_specs=pl.BlockSpec((seq, GROUP_W), lambda b, g: (b, g)),
        out_shape=jax.ShapeDtypeStruct((batch * seq, D_INNER), BF16),
        scratch_shapes=[pltpu.VMEM((seq, GROUP_W), BF16),
                        pltpu.VMEM((seq, D_STATE), BF16),
                        pltpu.VMEM((seq, D_STATE), BF16),
                        pltpu.VMEM((seq, GROUP_W), BF16),
                        pltpu.VMEM((D_STATE, GROUP_W), F32),
                        pltpu.VMEM((D_STATE, GROUP_W), F32)],
        compiler_params=pltpu.CompilerParams(
            dimension_semantics=("parallel", "parallel"), vmem_limit_bytes=VMEM_LIMIT_BYTES),
        name="ssd",
    )(proj, proj, proj, proj, rowpack, conv_w, conv_b, conv_w, conv_b, conv_w, conv_b,
      dskip_e, norm_w, h0)


CONF_LANE_BLOCK = 256


def _tail_kernel(yn_ref, glu_ref, cg_ref, gs_ref, gc_ref, x_ref, mod_ref, cw_ref, cb_ref,
                 lnw_ref, lnb_ref, wos_ref, woc_ref, wo_ref, fnw_ref, o_ref, upad_s, uc_s):
    tm = x_ref.shape[0]
    nseq = tm // GRID_W

    @pl.when(pl.program_id(0) == 0)
    def _():
        upad_s[...] = jnp.zeros_like(upad_s)

    glu = glu_ref[...].astype(F32)
    u = glu[:, :D_CONF] * _sigmoid(glu[:, D_CONF:])
    for q in range(nseq):
        upad_s[pl.ds(q * CONF_SLOT + CONF_PAD_ROWS, GRID_W), :] = u[q * GRID_W:(q + 1) * GRID_W]
    first_tap = CONF_PAD_ROWS - CONF_KERNEL // 2
    for q in range(nseq):
        for j in range(D_CONF // CONF_LANE_BLOCK):
            lanes = pl.ds(j * CONF_LANE_BLOCK, CONF_LANE_BLOCK)
            acc = jnp.broadcast_to(cb_ref[:, lanes], (GRID_W, CONF_LANE_BLOCK))
            for k in range(CONF_KERNEL):
                tap = upad_s[pl.ds(q * CONF_SLOT + first_tap + k, GRID_W), lanes]
                acc = acc + tap * cw_ref[k:k + 1, lanes]
            uc_s[pl.ds(q * GRID_W, GRID_W), lanes] = acc
    uc = uc_s[...]
    mu = jnp.mean(uc, axis=-1, keepdims=True)
    dev = uc - mu
    var = jnp.mean(dev * dev, axis=-1, keepdims=True)
    ln = dev * lax.rsqrt(var + EPS) * lnw_ref[...] + lnb_ref[...]
    u2 = _silu(ln) * _silu(cg_ref[...].astype(F32))
    branch_conf = jnp.dot(u2.astype(BF16), woc_ref[...], preferred_element_type=F32)
    branch_ssm = jnp.dot(yn_ref[...], wos_ref[...], preferred_element_type=F32)
    merged = (_sigmoid(gs_ref[...].astype(F32)) * branch_ssm
              + _sigmoid(gc_ref[...].astype(F32)) * branch_conf)
    out = jnp.dot(merged.astype(BF16), wo_ref[...], preferred_element_type=F32)
    xn = x_ref[...] + mod_ref[0, 2:3, :] * out
    ms = jnp.mean(xn * xn, axis=-1, keepdims=True)
    o_ref[...] = xn * lax.rsqrt(ms + EPS) * fnw_ref[...]


def _tail(yn, proj, x2d, mod3, conf_w, conf_b, ln_w, ln_b, w_os, w_oc, w_o, fn_w, *, tm, seq):
    m, d = x2d.shape
    tiles_per_batch = seq // tm
    const = lambda i: (0, 0)
    return pl.pallas_call(
        _tail_kernel,
        grid=(m // tm,),
        in_specs=[pl.BlockSpec((tm, D_INNER), lambda i: (i, 0)),
                  pl.BlockSpec((tm, 2 * D_CONF), lambda i: (i, P_GLU // (2 * D_CONF))),
                  pl.BlockSpec((tm, D_CONF), lambda i: (i, P_CG // D_CONF)),
                  pl.BlockSpec((tm, D_MODEL), lambda i: (i, P_GATE // D_MODEL)),
                  pl.BlockSpec((tm, D_MODEL), lambda i: (i, P_GATE // D_MODEL + 1)),
                  pl.BlockSpec((tm, d), lambda i: (i, 0)),
                  pl.BlockSpec((1, 3, d), lambda i: (i // tiles_per_batch, 0, 0)),
                  pl.BlockSpec((CONF_KERNEL, D_CONF), const),
                  pl.BlockSpec((1, D_CONF), const),
                  pl.BlockSpec((1, D_CONF), const),
                  pl.BlockSpec((1, D_CONF), const),
                  pl.BlockSpec((D_INNER, D_MODEL), const),
                  pl.BlockSpec((D_CONF, D_MODEL), const),
                  pl.BlockSpec((D_MODEL, D_MODEL), const),
                  pl.BlockSpec((1, D_MODEL), const)],
        out_specs=pl.BlockSpec((tm, d), lambda i: (i, 0)),
        out_shape=jax.ShapeDtypeStruct((m, d), F32),
        scratch_shapes=[pltpu.VMEM((tm // GRID_W * CONF_SLOT, D_CONF), F32),
                        pltpu.VMEM((tm, D_CONF), F32)],
        compiler_params=pltpu.CompilerParams(
            dimension_semantics=("arbitrary",), vmem_limit_bytes=VMEM_LIMIT_BYTES),
        name="tail",
    )(yn, proj, proj, proj, proj, x2d, mod3, conf_w, conf_b, ln_w, ln_b, w_os, w_oc, w_o, fn_w)


def _head_perm():
    return np.array([d * N_HEADS + g * HPG + r
                     for g in range(N_GROUPS) for d in range(2) for r in range(HPG)])


def kernel(x, c, ctx, c_ctx, w_mod, b_mod, norm_w, w_in, ssm_conv_w, ssm_conv_b, dt_bias, a_log,
           d_skip, ssm_norm_w, w_out_ssm, conf_conv_w, conf_conv_b, conf_ln_w, conf_ln_b,
           w_out_conf, w_out, final_norm_w):
    batch, seq, d = x.shape
    ctx_len = ctx.shape[1]
    assert w_in.shape[0] == 1, "single trunk layer"
    assert d == D_MODEL and seq % CHUNK == 0 and ctx_len % CHUNK == 0 and batch + 1 <= 8

    w_in0 = w_in[0]
    w_main = jnp.concatenate([w_in0[:, :C_END], w_in0[:, DT_END:]], axis=1).astype(BF16)
    perm = _head_perm()
    pad_heads = DT_LANES - 2 * N_HEADS
    w_dt = jnp.pad(w_in0[:, C_END:DT_END][:, perm], ((0, 0), (0, pad_heads))).astype(BF16)
    bias_p = jnp.pad(dt_bias[0].reshape(-1)[perm], (0, pad_heads)).reshape(1, DT_LANES)
    alog_p = jnp.pad(a_log[0].reshape(-1)[perm], (0, pad_heads)).reshape(1, DT_LANES)
    conv_w = ssm_conv_w[0]
    conv_b = ssm_conv_b[0].reshape(1, C_END)
    dskip_e = jnp.repeat(d_skip[0], HEAD_DIM).reshape(1, D_INNER)
    nw2 = norm_w[0].reshape(1, d)

    c_all = jnp.concatenate([c, c_ctx[None, :], jnp.zeros((8 - batch - 1, d), F32)], axis=0)
    mod3 = _mod(c_all, w_mod[0], b_mod[0]).reshape(8, 3, d)

    ctx_rows = batch * ctx_len
    tm_ctx = min(1024, ctx_rows)
    proj_ctx, dtraw_ctx = _in_proj(ctx.reshape(ctx_rows, d), mod3, nw2, w_main, w_dt,
                                   tm=tm_ctx, tn=1024, n_col_tiles=B_END // 1024,
                                   mod_row=lambda i: batch)
    rp_ctx = _dt_prep(dtraw_ctx, bias_p, alog_p, batch, ctx_len)
    h0 = _ctx_states(proj_ctx, rp_ctx, conv_w, conv_b, batch, ctx_len)

    m = batch * seq
    x2d = x.reshape(m, d)
    tm = min(1024, seq)
    tiles_per_batch = seq // tm
    proj, dtraw = _in_proj(x2d, mod3, nw2, w_main, w_dt, tm=tm, tn=2816,
                           n_col_tiles=P_COLS // 2816, mod_row=lambda i: i // tiles_per_batch)
    rp = _dt_prep(dtraw, bias_p, alog_p, batch, seq)
    yn = _ssd(proj, rp, conv_w, conv_b, dskip_e, ssm_norm_w[0].reshape(1, D_INNER), h0, batch, seq)
    out = _tail(yn, proj, x2d, mod3, conf_conv_w[0], conf_conv_b[0].reshape(1, D_CONF),
                conf_ln_w[0].reshape(1, D_CONF), conf_ln_b[0].reshape(1, D_CONF),
                w_out_ssm[0].astype(BF16), w_out_conf[0].astype(BF16), w_out[0].astype(BF16),
                final_norm_w.reshape(1, d), tm=min(256, seq), seq=seq)
    return out.reshape(batch, seq, d)
```

```python
import functools

import numpy as np
import jax
import jax.numpy as jnp
from jax import lax
from jax.experimental import pallas as pl
from jax.experimental.pallas import tpu as pltpu

F32 = jnp.float32
BF16 = jnp.bfloat16

D_MODEL = 1024
GRID_W = 64
D_INNER = 2 * D_MODEL
HEAD_DIM = 64
N_HEADS = D_INNER // HEAD_DIM
N_GROUPS = 8
HPG = N_HEADS // N_GROUPS
D_STATE = 128
SSM_CONV = 4
CHUNK = 128
D_CONF = D_MODEL
CONF_KERNEL = 31
EPS = 1e-6

GN = N_GROUPS * D_STATE
X_END = D_INNER
B_END = X_END + GN
C_END = B_END + GN
DT_END = C_END + 2 * N_HEADS
Z_END = DT_END + D_INNER
GLU_END = Z_END + 2 * D_CONF
CG_END = GLU_END + D_CONF
IN_COLS = CG_END + 2 * D_MODEL

GROUP_W = HPG * HEAD_DIM
DT_LANES = 128
P_XBC, P_Z, P_GLU, P_CG, P_GATE = 0, C_END, C_END + D_INNER, C_END + D_INNER + 2 * D_CONF, C_END + D_INNER + 3 * D_CONF
P_COLS = IN_COLS - 2 * N_HEADS

SUBLANES = 8
HALO = 16
CONF_PAD_ROWS = 16
CONF_SLOT = GRID_W + 2 * CONF_PAD_ROWS
VMEM_LIMIT_BYTES = 56 * 1024 * 1024


def _sigmoid(v):
    return 1.0 / (1.0 + jnp.exp(-v))


def _silu(v):
    return v * _sigmoid(v)


def _mod_kernel(c_ref, w_ref, b_ref, o_ref):
    s = _silu(c_ref[...])
    o_ref[...] = jnp.dot(s, w_ref[...], preferred_element_type=F32,
                         precision=lax.Precision.HIGHEST) + b_ref[...]


def _mod(c_all, w_mod, b_mod):
    rows, d = c_all.shape
    n = w_mod.shape[1]
    tn = 1024
    return pl.pallas_call(
        _mod_kernel,
        grid=(n // tn,),
        in_specs=[pl.BlockSpec((rows, d), lambda j: (0, 0)),
                  pl.BlockSpec((d, tn), lambda j: (0, j)),
                  pl.BlockSpec((1, tn), lambda j: (0, j))],
        out_specs=pl.BlockSpec((rows, tn), lambda j: (0, j)),
        out_shape=jax.ShapeDtypeStruct((rows, n), F32),
        name="mod",
    )(c_all, w_mod, b_mod.reshape(1, n))


def _inproj_kernel(x_ref, mod_ref, nw_ref, w_ref, wdt_ref, o_ref, dt_ref, h_ref):
    @pl.when(pl.program_id(1) == 0)
    def _():
        x = x_ref[...]
        ms = jnp.mean(x * x, axis=-1, keepdims=True)
        y = x * lax.rsqrt(ms + EPS) * nw_ref[...]
        h = y * (1.0 + mod_ref[0, 1:2, :]) + mod_ref[0, 0:1, :]
        hb = h.astype(BF16)
        h_ref[...] = hb
        dt_ref[...] = jnp.dot(hb, wdt_ref[...], preferred_element_type=F32)

    o_ref[...] = jnp.dot(h_ref[...], w_ref[...], preferred_element_type=F32).astype(BF16)


def _in_proj(x2d, mod3, norm_w, w_main, w_dt, *, tm, tn, n_col_tiles, mod_row):
    m, d = x2d.shape
    return pl.pallas_call(
        _inproj_kernel,
        grid=(m // tm, n_col_tiles),
        in_specs=[pl.BlockSpec((tm, d), lambda i, j: (i, 0)),
                  pl.BlockSpec((1, 3, d), lambda i, j: (mod_row(i), 0, 0)),
                  pl.BlockSpec((1, d), lambda i, j: (0, 0)),
                  pl.BlockSpec((d, tn), lambda i, j: (0, j)),
                  pl.BlockSpec((d, DT_LANES), lambda i, j: (0, 0))],
        out_specs=[pl.BlockSpec((tm, tn), lambda i, j: (i, j)),
                   pl.BlockSpec((tm, DT_LANES), lambda i, j: (i, 0))],
        out_shape=[jax.ShapeDtypeStruct((m, n_col_tiles * tn), BF16),
                   jax.ShapeDtypeStruct((m, DT_LANES), F32)],
        scratch_shapes=[pltpu.VMEM((tm, d), BF16)],
        compiler_params=pltpu.CompilerParams(
            dimension_semantics=("parallel", "arbitrary"), vmem_limit_bytes=VMEM_LIMIT_BYTES),
        name="in_proj",
    )(x2d, mod3, norm_w, w_main, w_dt)


N_PIECES = 3
HD = 2 * HPG
LHS_ONES = 0
LHS_LA = 32
ROW_WEND = N_PIECES * HD
ROWS_PER_GROUP = ROW_WEND + HD
MASK_BIG = 1e30
DT_FLOOR = 1e-37
LOG2E = 1.4426950408889634


def _decay_constants():
    k = np.arange(CHUNK)[:, None]
    s = np.arange(CHUNK)[None, :]
    rest = np.zeros((2 * CHUNK, HD * CHUNK), np.float32)
    expand = np.zeros((CHUNK, 2 * GROUP_W), np.float32)
    for h in range(HD):
        cols = slice(h * CHUNK, (h + 1) * CHUNK)
        hidden = (s > k) if h < HPG else (s < k)
        rest[CHUNK:, cols] = np.where(hidden, -MASK_BIG, 0.0)
        for j in range(N_PIECES):
            rest[LHS_LA + HD * j + h, cols] = 1.0
            expand[LHS_LA + HD * j + h, h * HEAD_DIM:(h + 1) * HEAD_DIM] = 1.0
    scatter = np.zeros((4 * DT_LANES, N_GROUPS * CHUNK), np.float32)
    for g in range(N_GROUPS):
        scatter[N_PIECES * DT_LANES, g * CHUNK + LHS_ONES:g * CHUNK + LHS_ONES + ROW_WEND] = 1.0
        for h in range(HD):
            for j in range(N_PIECES):
                scatter[j * DT_LANES + HD * g + h, g * CHUNK + LHS_LA + HD * j + h] = 1.0
    return (jnp.asarray(rest[ROWS_PER_GROUP:], BF16), jnp.asarray(expand, BF16),
            jnp.asarray(np.eye(CHUNK), BF16), jnp.asarray(scatter, BF16))


def _split3(v):
    p0 = v.astype(BF16)
    r1 = v - p0.astype(F32)
    p1 = r1.astype(BF16)
    p2 = (r1 - p1.astype(F32)).astype(BF16)
    return p0, p1, p2


def _dt_kernel(raw_ref, bias_ref, alog_ref, scatter_ref, lhsp_ref, rowpk_ref):
    n_chunks = raw_ref.shape[0] // CHUNK
    bias = bias_ref[...]
    a = -jnp.exp(alog_ref[...])
    row = lax.broadcasted_iota(jnp.int32, (CHUNK, CHUNK), 0)
    col = lax.broadcasted_iota(jnp.int32, (CHUNK, CHUNK), 1)
    tri_lo = (col <= row).astype(F32)
    tri_up = (col >= row).astype(F32)
    lane = lax.broadcasted_iota(jnp.int32, (1, DT_LANES), 1)
    is_fwd = (lane % HD) < HPG
    ones = jnp.ones((CHUNK, DT_LANES), BF16)
    for ci in range(n_chunks):
        rows = pl.ds(ci * CHUNK, CHUNK)
        v = raw_ref[rows, :] + bias
        dt = jnp.maximum(v, 0.0) + jnp.log1p(jnp.exp(-jnp.abs(v)))
        dta = dt * a
        la_f = jnp.dot(tri_lo, dta, preferred_element_type=F32, precision=lax.Precision.HIGHEST)
        la_b = jnp.dot(tri_up, dta, preferred_element_type=F32, precision=lax.Precision.HIGHEST)
        la = jnp.where(is_fwd, la_f, la_b)
        la_end = jnp.where(is_fwd, la[CHUNK - 1:CHUNK, :], la[0:1, :])
        wend = jnp.exp(la_end - la) * dt
        la2 = la * LOG2E
        nla = jnp.log2(jnp.maximum(dt, DT_FLOOR)) - la2
        lhs = jnp.concatenate(list(_split3(la2)) + [ones], axis=1)
        lhsp_ref[rows, :] = jnp.dot(lhs, scatter_ref[...],
                                    preferred_element_type=F32).astype(lhsp_ref.dtype)
        pieces = [p.astype(F32) for p in _split3(nla.T)] + [wend.T]
        for g in range(N_GROUPS):
            for j, val in enumerate(pieces):
                rowpk_ref[0, pl.ds(ROWS_PER_GROUP * g + HD * j, HD), rows] = val[HD * g:HD * (g + 1), :]


DT_CHUNKS_PER_STEP = 8


def _dt_prep(dt_raw, bias_p, alog_p, scatter, batch, seq):
    nc = seq // CHUNK
    per = min(DT_CHUNKS_PER_STEP, nc)
    steps = nc // per
    rows = per * CHUNK
    const = lambda b, c: (0, 0)
    return pl.pallas_call(
        _dt_kernel,
        grid=(batch, steps),
        in_specs=[pl.BlockSpec((rows, DT_LANES), lambda b, c: (b * steps + c, 0)),
                  pl.BlockSpec((1, DT_LANES), const),
                  pl.BlockSpec((1, DT_LANES), const),
                  pl.BlockSpec(scatter.shape, const)],
        out_specs=[pl.BlockSpec((rows, N_GROUPS * CHUNK), lambda b, c: (b * steps + c, 0)),
                   pl.BlockSpec((1, N_GROUPS * ROWS_PER_GROUP, rows), lambda b, c: (b, 0, c))],
        out_shape=[jax.ShapeDtypeStruct((batch * seq, N_GROUPS * CHUNK), BF16),
                   jax.ShapeDtypeStruct((batch, N_GROUPS * ROWS_PER_GROUP, seq), F32)],
        compiler_params=pltpu.CompilerParams(dimension_semantics=("parallel", "parallel")),
        name="dt_prep",
    )(dt_raw, bias_p, alog_p, scatter)


def _chunk_start(c):
    return c * CHUNK if isinstance(c, int) else pl.multiple_of(c * CHUNK, CHUNK)


def _conv_silu_chunk(refs, c, nc, w, b):
    seq = refs[0].shape[0]
    r0 = _chunk_start(c)
    if isinstance(c, int):
        p0, n0 = max(r0 - HALO, 0), min(r0 + CHUNK, seq - HALO)
    else:
        p0 = pl.multiple_of(jnp.maximum(r0 - HALO, 0), HALO)
        n0 = pl.multiple_of(jnp.minimum(r0 + CHUNK, seq - HALO), HALO)

    def rows(start, size):
        return jnp.concatenate([ref[pl.ds(start, size), :] for ref in refs], axis=1).astype(F32)

    prev, main, nxt = rows(p0, HALO), rows(r0, CHUNK), rows(n0, HALO)
    prev = jnp.where(c > 0, prev, 0.0)
    nxt = jnp.where(c < nc - 1, nxt, 0.0)
    ext = jnp.concatenate([prev, main, nxt], axis=0)
    total = CHUNK + 2 * HALO
    xm2 = pltpu.roll(ext, 2, 0)[HALO:HALO + CHUNK]
    xm1 = pltpu.roll(ext, 1, 0)[HALO:HALO + CHUNK]
    xp1 = pltpu.roll(ext, total - 1, 0)[HALO:HALO + CHUNK]
    y = w[0:1] * xm2 + w[1:2] * xm1 + w[2:3] * main + w[3:4] * xp1 + b
    return _silu(y)


def _chunk_operands(lhsp_ref, rowpk_ref, c):
    r0 = _chunk_start(c)
    return lhsp_ref[pl.ds(r0, CHUNK), :], rowpk_ref[0, :, pl.ds(r0, CHUNK)]


def _head_decay(lt, expand_ref, direction=None):
    if direction is None:
        ex = expand_ref[...]
    else:
        ex = expand_ref[:, direction * GROUP_W:(direction + 1) * GROUP_W]
    return jnp.exp2(jnp.dot(lt, ex, preferred_element_type=F32))


def _xs_blockdiag(xs_bf):
    lane = lax.broadcasted_iota(jnp.int32, xs_bf.shape, 1)
    zero = jnp.zeros_like(xs_bf)
    return jnp.concatenate(
        [jnp.where((lane >= r * HEAD_DIM) & (lane < (r + 1) * HEAD_DIM), xs_bf, zero)
         for r in range(HPG)], axis=0)


def _bt_weighted(bt_bf, rp, direction):
    btf = bt_bf.astype(F32)
    base = ROW_WEND + direction * HPG
    return jnp.concatenate([(btf * rp[base + r:base + r + 1, :]).astype(BF16) for r in range(HPG)],
                           axis=1)


def _ctx_kernel(xs_ref, b_ref, lhsp_ref, rowpk_ref, cwx_ref, cbx_ref, cwb_ref, cbb_ref,
                expand_ref, o_ref):
    nc = xs_ref.shape[0] // CHUNK
    cw = jnp.concatenate([cwx_ref[...], cwb_ref[...]], axis=1)
    cb = jnp.concatenate([cbx_ref[...], cbb_ref[...]], axis=1)
    contrib, decay = [], []
    for c in range(nc):
        xb = _conv_silu_chunk((xs_ref, b_ref), c, nc, cw, cb)
        xs_bf = xb[:, :GROUP_W].astype(BF16)
        bt = xb[:, GROUP_W:].T.astype(BF16)
        lt, rp = _chunk_operands(lhsp_ref, rowpk_ref, c)
        lhs = jnp.concatenate([_bt_weighted(bt, rp, 0), _bt_weighted(bt, rp, 1)], axis=0)
        contrib.append(jnp.dot(lhs, _xs_blockdiag(xs_bf), preferred_element_type=F32))
        decay.append(_head_decay(lt, expand_ref))
    hf = jnp.zeros((D_STATE, GROUP_W), F32)
    for c in range(nc):
        hf = hf * decay[c][CHUNK - 1:CHUNK, :GROUP_W] + contrib[c][:D_STATE]
    hb = jnp.zeros((D_STATE, GROUP_W), F32)
    for c in reversed(range(nc)):
        hb = hb * decay[c][0:1, GROUP_W:] + contrib[c][D_STATE:]
    o_ref[0, 0, 0] = hf
    o_ref[0, 0, 1] = hb


def _ctx_states(proj_ctx, lhsp, rowpk, conv_w, conv_b, expand, batch, seq):
    bcol = X_END // D_STATE
    const = lambda b, g: (0, 0)
    return pl.pallas_call(
        _ctx_kernel,
        grid=(batch, N_GROUPS),
        in_specs=[pl.BlockSpec((seq, GROUP_W), lambda b, g: (b, g)),
                  pl.BlockSpec((seq, D_STATE), lambda b, g: (b, bcol + g)),
                  pl.BlockSpec((seq, CHUNK), lambda b, g: (b, g)),
                  pl.BlockSpec((1, ROWS_PER_GROUP, seq), lambda b, g: (b, g, 0)),
                  pl.BlockSpec((SSM_CONV, GROUP_W), lambda b, g: (0, g)),
                  pl.BlockSpec((1, GROUP_W), lambda b, g: (0, g)),
                  pl.BlockSpec((SSM_CONV, D_STATE), lambda b, g: (0, bcol + g)),
                  pl.BlockSpec((1, D_STATE), lambda b, g: (0, bcol + g)),
                  pl.BlockSpec(expand.shape, const)],
        out_specs=pl.BlockSpec((1, 1, 2, D_STATE, GROUP_W), lambda b, g: (b, g, 0, 0, 0)),
        out_shape=jax.ShapeDtypeStruct((batch, N_GROUPS, 2, D_STATE, GROUP_W), F32),
        compiler_params=pltpu.CompilerParams(dimension_semantics=("parallel", "parallel")),
        name="ctx_states",
    )(proj_ctx, proj_ctx, lhsp, rowpk, conv_w, conv_b, conv_w, conv_b, expand)


def _ssd_kernel(xs_ref, b_ref, c_ref, z_ref, lhsp_ref, rowpk_ref, cwx_ref, cbx_ref, cwb_ref,
                cbb_ref, cwc_ref, cbc_ref, dsk_ref, nw_ref, h0_ref, rest_ref, expand_ref, eye_ref,
                o_ref, cx_s, cbt_s, cc_s, stash_s, hf_s, hb_s):
    seq = xs_ref.shape[0]
    nc = seq // CHUNK
    cw = jnp.concatenate([cwx_ref[...], cwb_ref[...], cwc_ref[...]], axis=1)
    cb = jnp.concatenate([cbx_ref[...], cbb_ref[...], cbc_ref[...]], axis=1)
    hf_s[...] = h0_ref[0, 0, 0]
    hb_s[...] = h0_ref[0, 0, 1]

    def back_body(i, carry):
        c = nc - 1 - i
        r0 = _chunk_start(c)
        xbc = _conv_silu_chunk((xs_ref, b_ref, c_ref), c, nc, cw, cb)
        xs_bf = xbc[:, :GROUP_W].astype(BF16)
        bt = xbc[:, GROUP_W:GROUP_W + D_STATE].T.astype(BF16)
        cx_s[pl.ds(r0, CHUNK), :] = xs_bf
        cbt_s[pl.ds(r0, CHUNK), :] = bt
        cc_s[pl.ds(r0, CHUNK), :] = xbc[:, GROUP_W + D_STATE:].astype(BF16)
        lt, rp = _chunk_operands(lhsp_ref, rowpk_ref, c)
        hb = hb_s[...]
        stash_s[pl.ds(r0, CHUNK), :] = hb.astype(BF16)
        dec = _head_decay(lt, expand_ref, 1)[0:1, :]
        contrib = jnp.dot(_bt_weighted(bt, rp, 1), _xs_blockdiag(xs_bf), preferred_element_type=F32)
        hb_s[...] = hb * dec + contrib
        return carry

    lax.fori_loop(0, nc, back_body, 0, unroll=2)

    prow = lax.broadcasted_iota(jnp.int32, (ROWS_PER_GROUP, HD * CHUNK), 0)
    pcol = lax.broadcasted_iota(jnp.int32, (ROWS_PER_GROUP, HD * CHUNK), 1)
    own_block = (prow < ROW_WEND) & (prow % HD == pcol // CHUNK)
    dsk, nw = dsk_ref[...], nw_ref[...]

    def fwd_body(c, carry):
        r0 = _chunk_start(c)
        xs_bf = cx_s[pl.ds(r0, CHUNK), :]
        bt = cbt_s[pl.ds(r0, CHUNK), :]
        cc = cc_s[pl.ds(r0, CHUNK), :]
        lt, rp = _chunk_operands(lhsp_ref, rowpk_ref, c)
        scores = jnp.dot(cc, bt, preferred_element_type=F32)
        rows = jnp.concatenate([rp.astype(BF16)] * HD, axis=1)
        rows = jnp.where(own_block, rows, jnp.zeros_like(rows))
        seg = jnp.dot(jnp.concatenate([lt, eye_ref[...]], axis=1),
                      jnp.concatenate([rows, rest_ref[...]], axis=0), preferred_element_type=F32)
        decay_dt = jnp.exp2(seg)
        mix = jnp.concatenate(
            [(scores * (decay_dt[:, r * CHUNK:(r + 1) * CHUNK]
                        + decay_dt[:, (HPG + r) * CHUNK:(HPG + r + 1) * CHUNK])).astype(BF16)
             for r in range(HPG)], axis=1)
        lhs = jnp.concatenate([mix, _bt_weighted(bt, rp, 0)], axis=0)
        prod = jnp.dot(lhs, _xs_blockdiag(xs_bf), preferred_element_type=F32)
        hf = hf_s[...]
        e = _head_decay(lt, expand_ref)
        states = jnp.concatenate([hf.astype(BF16), stash_s[pl.ds(r0, CHUNK), :]], axis=1)
        carried = jnp.dot(cc, states, preferred_element_type=F32) * e
        y = dsk * xs_bf.astype(F32) + prod[:CHUNK] + carried[:, :GROUP_W] + carried[:, GROUP_W:]
        hf_s[...] = hf * e[CHUNK - 1:CHUNK, :GROUP_W] + prod[CHUNK:]
        y = y * _silu(z_ref[pl.ds(r0, CHUNK), :].astype(F32))
        ms = jnp.mean(y * y, axis=-1, keepdims=True)
        o_ref[pl.ds(r0, CHUNK), :] = (y * lax.rsqrt(ms + EPS) * nw).astype(o_ref.dtype)
        return carry

    lax.fori_loop(0, nc, fwd_body, 0, unroll=4)


def _ssd(proj, lhsp, rowpk, conv_w, conv_b, dskip_e, norm_w, h0, rest, expand, eye, batch, seq):
    bcol = X_END // D_STATE
    ccol = B_END // D_STATE
    zcol = P_Z // GROUP_W
    gmap = lambda b, g: (0, g)
    const = lambda b, g: (0, 0)
    return pl.pallas_call(
        _ssd_kernel,
        grid=(batch, N_GROUPS),
        in_specs=[pl.BlockSpec((seq, GROUP_W), lambda b, g: (b, g)),
                  pl.BlockSpec((seq, D_STATE), lambda b, g: (b, bcol + g)),
                  pl.BlockSpec((seq, D_STATE), lambda b, g: (b, ccol + g)),
                  pl.BlockSpec((seq, GROUP_W), lambda b, g: (b, zcol + g)),
                  pl.BlockSpec((seq, CHUNK), lambda b, g: (b, g)),
                  pl.BlockSpec((1, ROWS_PER_GROUP, seq), lambda b, g: (b, g, 0)),
                  pl.BlockSpec((SSM_CONV, GROUP_W), gmap),
                  pl.BlockSpec((1, GROUP_W), gmap),
                  pl.BlockSpec((SSM_CONV, D_STATE), lambda b, g: (0, bcol + g)),
                  pl.BlockSpec((1, D_STATE), lambda b, g: (0, bcol + g)),
                  pl.BlockSpec((SSM_CONV, D_STATE), lambda b, g: (0, ccol + g)),
                  pl.BlockSpec((1, D_STATE), lambda b, g: (0, ccol + g)),
                  pl.BlockSpec((1, GROUP_W), gmap),
                  pl.BlockSpec((1, GROUP_W), gmap),
                  pl.BlockSpec((1, 1, 2, D_STATE, GROUP_W), lambda b, g: (b, g, 0, 0, 0)),
                  pl.BlockSpec(rest.shape, const),
                  pl.BlockSpec(expand.shape, const),
                  pl.BlockSpec(eye.shape, const)],
        out_specs=pl.BlockSpec((seq, GROUP_W), lambda b, g: (b, g)),
        out_shape=jax.ShapeDtypeStruct((batch * seq, D_INNER), BF16),
        scratch_shapes=[pltpu.VMEM((seq, GROUP_W), BF16),
                        pltpu.VMEM((seq, D_STATE), BF16),
                        pltpu.VMEM((seq, D_STATE), BF16),
                        pltpu.VMEM((seq, GROUP_W), BF16),
                        pltpu.VMEM((D_STATE, GROUP_W), F32),
                        pltpu.VMEM((D_STATE, GROUP_W), F32)],
        compiler_params=pltpu.CompilerParams(
            dimension_semantics=("parallel", "parallel"), vmem_limit_bytes=VMEM_LIMIT_BYTES),
        name="ssd",
    )(proj, proj, proj, proj, lhsp, rowpk, conv_w, conv_b, conv_w, conv_b, conv_w, conv_b,
      dskip_e, norm_w, h0, rest, expand, eye)


CONF_LANE_BLOCK = 128


def _tail_kernel(yn_ref, glu_ref, cg_ref, gs_ref, gc_ref, x_ref, mod_ref, cw_ref, cb_ref,
                 lnw_ref, lnb_ref, wos_ref, woc_ref, wo_ref, fnw_ref, o_ref, upad_s, uc_s):
    tm = x_ref.shape[0]
    nseq = tm // GRID_W

    @pl.when(pl.program_id(0) == 0)
    def _():
        upad_s[...] = jnp.zeros_like(upad_s)

    glu = glu_ref[...].astype(F32)
    u = glu[:, :D_CONF] * _sigmoid(glu[:, D_CONF:])
    for q in range(nseq):
        upad_s[pl.ds(q * CONF_SLOT + CONF_PAD_ROWS, GRID_W), :] = u[q * GRID_W:(q + 1) * GRID_W]
    first_tap = CONF_PAD_ROWS - CONF_KERNEL // 2
    for q in range(nseq):
        for j in range(D_CONF // CONF_LANE_BLOCK):
            lanes = pl.ds(j * CONF_LANE_BLOCK, CONF_LANE_BLOCK)
            acc = jnp.broadcast_to(cb_ref[:, lanes], (GRID_W, CONF_LANE_BLOCK))
            padded = upad_s[pl.ds(q * CONF_SLOT, CONF_SLOT), lanes]
            for r in range(SUBLANES):
                shifted = padded if r == 0 else pltpu.roll(padded, CONF_SLOT - r, 0)
                for a in range((CONF_SLOT - GRID_W) // SUBLANES):
                    k = SUBLANES * a + r - first_tap
                    if 0 <= k < CONF_KERNEL:
                        acc = acc + shifted[SUBLANES * a:SUBLANES * a + GRID_W] * cw_ref[k:k + 1, lanes]
            uc_s[pl.ds(q * GRID_W, GRID_W), lanes] = acc
    uc = uc_s[...]
    mu = jnp.mean(uc, axis=-1, keepdims=True)
    dev = uc - mu
    var = jnp.mean(dev * dev, axis=-1, keepdims=True)
    ln = dev * lax.rsqrt(var + EPS) * lnw_ref[...] + lnb_ref[...]
    u2 = _silu(ln) * _silu(cg_ref[...].astype(F32))
    branch_conf = jnp.dot(u2.astype(BF16), woc_ref[...], preferred_element_type=F32)
    branch_ssm = jnp.dot(yn_ref[...], wos_ref[...], preferred_element_type=F32)
    merged = (_sigmoid(gs_ref[...].astype(F32)) * branch_ssm
              + _sigmoid(gc_ref[...].astype(F32)) * branch_conf)
    out = jnp.dot(merged.astype(BF16), wo_ref[...], preferred_element_type=F32)
    xn = x_ref[...] + mod_ref[0, 2:3, :] * out
    ms = jnp.mean(xn * xn, axis=-1, keepdims=True)
    o_ref[...] = xn * lax.rsqrt(ms + EPS) * fnw_ref[...]


def _tail(yn, proj, x2d, mod3, conf_w, conf_b, ln_w, ln_b, w_os, w_oc, w_o, fn_w, *, tm, seq):
    m, d = x2d.shape
    tiles_per_batch = seq // tm
    const = lambda i: (0, 0)
    return pl.pallas_call(
        _tail_kernel,
        grid=(m // tm,),
        in_specs=[pl.BlockSpec((tm, D_INNER), lambda i: (i, 0)),
                  pl.BlockSpec((tm, 2 * D_CONF), lambda i: (i, P_GLU // (2 * D_CONF))),
                  pl.BlockSpec((tm, D_CONF), lambda i: (i, P_CG // D_CONF)),
                  pl.BlockSpec((tm, D_MODEL), lambda i: (i, P_GATE // D_MODEL)),
                  pl.BlockSpec((tm, D_MODEL), lambda i: (i, P_GATE // D_MODEL + 1)),
                  pl.BlockSpec((tm, d), lambda i: (i, 0)),
                  pl.BlockSpec((1, 3, d), lambda i: (i // tiles_per_batch, 0, 0)),
                  pl.BlockSpec((CONF_KERNEL, D_CONF), const),
                  pl.BlockSpec((1, D_CONF), const),
                  pl.BlockSpec((1, D_CONF), const),
                  pl.BlockSpec((1, D_CONF), const),
                  pl.BlockSpec((D_INNER, D_MODEL), const),
                  pl.BlockSpec((D_CONF, D_MODEL), const),
                  pl.BlockSpec((D_MODEL, D_MODEL), const),
                  pl.BlockSpec((1, D_MODEL), const)],
        out_specs=pl.BlockSpec((tm, d), lambda i: (i, 0)),
        out_shape=jax.ShapeDtypeStruct((m, d), F32),
        scratch_shapes=[pltpu.VMEM((tm // GRID_W * CONF_SLOT, D_CONF), F32),
                        pltpu.VMEM((tm, D_CONF), F32)],
        compiler_params=pltpu.CompilerParams(
            dimension_semantics=("arbitrary",), vmem_limit_bytes=VMEM_LIMIT_BYTES),
        name="tail",
    )(yn, proj, proj, proj, proj, x2d, mod3, conf_w, conf_b, ln_w, ln_b, w_os, w_oc, w_o, fn_w)


def _head_perm():
    return np.array([d * N_HEADS + g * HPG + r
                     for g in range(N_GROUPS) for d in range(2) for r in range(HPG)])


def kernel(x, c, ctx, c_ctx, w_mod, b_mod, norm_w, w_in, ssm_conv_w, ssm_conv_b, dt_bias, a_log,
           d_skip, ssm_norm_w, w_out_ssm, conf_conv_w, conf_conv_b, conf_ln_w, conf_ln_b,
           w_out_conf, w_out, final_norm_w):
    batch, seq, d = x.shape
    ctx_len = ctx.shape[1]
    assert w_in.shape[0] == 1, "single trunk layer"
    assert d == D_MODEL and seq % CHUNK == 0 and ctx_len % CHUNK == 0 and batch + 1 <= 8

    w_in0 = w_in[0]
    w_main = jnp.concatenate([w_in0[:, :C_END], w_in0[:, DT_END:]], axis=1).astype(BF16)
    perm = _head_perm()
    pad_heads = DT_LANES - 2 * N_HEADS
    w_dt = jnp.pad(w_in0[:, C_END:DT_END][:, perm], ((0, 0), (0, pad_heads))).astype(BF16)
    bias_p = jnp.pad(dt_bias[0].reshape(-1)[perm], (0, pad_heads)).reshape(1, DT_LANES)
    alog_p = jnp.pad(a_log[0].reshape(-1)[perm], (0, pad_heads)).reshape(1, DT_LANES)
    conv_w = ssm_conv_w[0]
    conv_b = ssm_conv_b[0].reshape(1, C_END)
    dskip_e = jnp.repeat(d_skip[0], HEAD_DIM).reshape(1, D_INNER)
    nw2 = norm_w[0].reshape(1, d)
    rest, expand, eye, scatter = _decay_constants()

    c_all = jnp.concatenate([c, c_ctx[None, :], jnp.zeros((8 - batch - 1, d), F32)], axis=0)
    mod3 = _mod(c_all, w_mod[0], b_mod[0]).reshape(8, 3, d)

    ctx_rows = batch * ctx_len
    tm_ctx = min(1024, ctx_rows)
    proj_ctx, dtraw_ctx = _in_proj(ctx.reshape(ctx_rows, d), mod3, nw2, w_main, w_dt,
                                   tm=tm_ctx, tn=1024, n_col_tiles=B_END // 1024,
                                   mod_row=lambda i: batch)
    lhsp_ctx, rowpk_ctx = _dt_prep(dtraw_ctx, bias_p, alog_p, scatter, batch, ctx_len)
    h0 = _ctx_states(proj_ctx, lhsp_ctx, rowpk_ctx, conv_w, conv_b, expand, batch, ctx_len)

    m = batch * seq
    x2d = x.reshape(m, d)
    tm = min(1024, seq)
    tiles_per_batch = seq // tm
    proj, dtraw = _in_proj(x2d, mod3, nw2, w_main, w_dt, tm=tm, tn=2816,
                           n_col_tiles=P_COLS // 2816, mod_row=lambda i: i // tiles_per_batch)
    lhsp, rowpk = _dt_prep(dtraw, bias_p, alog_p, scatter, batch, seq)
    yn = _ssd(proj, lhsp, rowpk, conv_w, conv_b, dskip_e, ssm_norm_w[0].reshape(1, D_INNER), h0,
              rest, expand, eye, batch, seq)
    out = _tail(yn, proj, x2d, mod3, conf_conv_w[0], conf_conv_b[0].reshape(1, D_CONF),
                conf_ln_w[0].reshape(1, D_CONF), conf_ln_b[0].reshape(1, D_CONF),
                w_out_ssm[0].astype(BF16), w_out_conf[0].astype(BF16), w_out[0].astype(BF16),
                final_norm_w.reshape(1, d), tm=min(256, seq), seq=seq)
    return out.reshape(batch, seq, d)
```

```python
import functools

import numpy as np
import jax
import jax.numpy as jnp
from jax import lax
from jax.experimental import pallas as pl
from jax.experimental.pallas import tpu as pltpu

F32 = jnp.float32
BF16 = jnp.bfloat16

D_MODEL = 1024
GRID_W = 64
D_INNER = 2 * D_MODEL
HEAD_DIM = 64
N_HEADS = D_INNER // HEAD_DIM
N_GROUPS = 8
HPG = N_HEADS // N_GROUPS
D_STATE = 128
SSM_CONV = 4
CHUNK = 128
D_CONF = D_MODEL
CONF_KERNEL = 31
EPS = 1e-6

GN = N_GROUPS * D_STATE
X_END = D_INNER
B_END = X_END + GN
C_END = B_END + GN
DT_END = C_END + 2 * N_HEADS
Z_END = DT_END + D_INNER
GLU_END = Z_END + 2 * D_CONF
CG_END = GLU_END + D_CONF
IN_COLS = CG_END + 2 * D_MODEL

GROUP_W = HPG * HEAD_DIM
DT_LANES = 128
P_Z = C_END
P_COLS = C_END + D_INNER
IN_PROJ_TN = 2048

SUBLANES = 8
HALO = 16
CONF_PAD_ROWS = 16
CONF_SLOT = GRID_W + 2 * CONF_PAD_ROWS
VMEM_LIMIT_BYTES = 56 * 1024 * 1024


def _sigmoid(v):
    return 1.0 / (1.0 + jnp.exp(-v))


def _silu(v):
    return v * _sigmoid(v)


def _mod_kernel(c_ref, w_ref, b_ref, o_ref):
    s = _silu(c_ref[...])
    o_ref[...] = jnp.dot(s, w_ref[...], preferred_element_type=F32,
                         precision=lax.Precision.HIGHEST) + b_ref[...]


def _mod(c_all, w_mod, b_mod):
    rows, d = c_all.shape
    n = w_mod.shape[1]
    tn = 1024
    return pl.pallas_call(
        _mod_kernel,
        grid=(n // tn,),
        in_specs=[pl.BlockSpec((rows, d), lambda j: (0, 0)),
                  pl.BlockSpec((d, tn), lambda j: (0, j)),
                  pl.BlockSpec((1, tn), lambda j: (0, j))],
        out_specs=pl.BlockSpec((rows, tn), lambda j: (0, j)),
        out_shape=jax.ShapeDtypeStruct((rows, n), F32),
        name="mod",
    )(c_all, w_mod, b_mod.reshape(1, n))


def _modulated_norm(x, nw, mod_ref):
    ms = jnp.mean(x * x, axis=-1, keepdims=True)
    y = x * lax.rsqrt(ms + EPS) * nw
    return y * (1.0 + mod_ref[0, 1:2, :]) + mod_ref[0, 0:1, :]


def _inproj_kernel(x_ref, mod_ref, nw_ref, w_ref, wdt_ref, o_ref, dt_ref, h_ref, *, n_plain_tiles):
    j = pl.program_id(1)

    @pl.when(j == 0)
    def _():
        hb = _modulated_norm(x_ref[...], nw_ref[...], mod_ref).astype(BF16)
        h_ref[...] = hb
        dt_ref[...] = jnp.dot(hb, wdt_ref[...], preferred_element_type=F32)

    @pl.when(j < n_plain_tiles)
    def _():
        o_ref[...] = jnp.dot(h_ref[...], w_ref[...], preferred_element_type=F32).astype(BF16)

    @pl.when(j >= n_plain_tiles)
    def _():
        o_ref[...] = _silu(jnp.dot(h_ref[...], w_ref[...], preferred_element_type=F32)).astype(BF16)


def _in_proj(x2d, mod3, norm_w, w_main, w_dt, *, tm, tn, n_col_tiles, n_plain_tiles, mod_row):
    m, d = x2d.shape
    return pl.pallas_call(
        functools.partial(_inproj_kernel, n_plain_tiles=n_plain_tiles),
        grid=(m // tm, n_col_tiles),
        in_specs=[pl.BlockSpec((tm, d), lambda i, j: (i, 0)),
                  pl.BlockSpec((1, 3, d), lambda i, j: (mod_row(i), 0, 0)),
                  pl.BlockSpec((1, d), lambda i, j: (0, 0)),
                  pl.BlockSpec((d, tn), lambda i, j: (0, j)),
                  pl.BlockSpec((d, DT_LANES), lambda i, j: (0, 0))],
        out_specs=[pl.BlockSpec((tm, tn), lambda i, j: (i, j)),
                   pl.BlockSpec((tm, DT_LANES), lambda i, j: (i, 0))],
        out_shape=[jax.ShapeDtypeStruct((m, n_col_tiles * tn), BF16),
                   jax.ShapeDtypeStruct((m, DT_LANES), F32)],
        scratch_shapes=[pltpu.VMEM((tm, d), BF16)],
        compiler_params=pltpu.CompilerParams(
            dimension_semantics=("parallel", "arbitrary"), vmem_limit_bytes=VMEM_LIMIT_BYTES),
        name="in_proj",
    )(x2d, mod3, norm_w, w_main, w_dt)


N_PIECES = 3
HD = 2 * HPG
LHS_ONES = 0
LHS_LA = 32
ROW_WEND = N_PIECES * HD
ROWS_PER_GROUP = ROW_WEND + HD
MASK_BIG = 1e30
DT_FLOOR = 1e-37
LOG2E = 1.4426950408889634


def _decay_constants():
    k = np.arange(CHUNK)[:, None]
    s = np.arange(CHUNK)[None, :]
    rest = np.zeros((2 * CHUNK, HD * CHUNK), np.float32)
    expand = np.zeros((CHUNK, 2 * GROUP_W), np.float32)
    for h in range(HD):
        cols = slice(h * CHUNK, (h + 1) * CHUNK)
        hidden = (s > k) if h < HPG else (s < k)
        rest[CHUNK:, cols] = np.where(hidden, -MASK_BIG, 0.0)
        for j in range(N_PIECES):
            rest[LHS_LA + HD * j + h, cols] = 1.0
            expand[LHS_LA + HD * j + h, h * HEAD_DIM:(h + 1) * HEAD_DIM] = 1.0
    scatter = np.zeros((4 * DT_LANES, N_GROUPS * CHUNK), np.float32)
    for g in range(N_GROUPS):
        scatter[N_PIECES * DT_LANES, g * CHUNK + LHS_ONES:g * CHUNK + LHS_ONES + ROW_WEND] = 1.0
        for h in range(HD):
            for j in range(N_PIECES):
                scatter[j * DT_LANES + HD * g + h, g * CHUNK + LHS_LA + HD * j + h] = 1.0
    return (jnp.asarray(rest[ROWS_PER_GROUP:], BF16), jnp.asarray(expand, BF16),
            jnp.asarray(np.eye(CHUNK), BF16), jnp.asarray(scatter, BF16))


def _split3(v):
    p0 = v.astype(BF16)
    r1 = v - p0.astype(F32)
    p1 = r1.astype(BF16)
    p2 = (r1 - p1.astype(F32)).astype(BF16)
    return p0, p1, p2


def _dt_kernel(raw_ref, bias_ref, alog_ref, scatter_ref, lhsp_ref, rowpk_ref):
    n_chunks = raw_ref.shape[0] // CHUNK
    bias = bias_ref[...]
    a = -jnp.exp(alog_ref[...])
    row = lax.broadcasted_iota(jnp.int32, (CHUNK, CHUNK), 0)
    col = lax.broadcasted_iota(jnp.int32, (CHUNK, CHUNK), 1)
    tri_lo = (col <= row).astype(BF16)
    tri_up = (col >= row).astype(BF16)
    lane = lax.broadcasted_iota(jnp.int32, (1, DT_LANES), 1)
    is_fwd = (lane % HD) < HPG
    ones = jnp.ones((CHUNK, DT_LANES), BF16)

    def cumulate(tri, pieces):
        acc = jnp.dot(tri, pieces, preferred_element_type=F32)
        return acc[:, :DT_LANES] + acc[:, DT_LANES:2 * DT_LANES] + acc[:, 2 * DT_LANES:]

    for ci in range(n_chunks):
        rows = pl.ds(ci * CHUNK, CHUNK)
        v = raw_ref[rows, :] + bias
        dt = jnp.maximum(v, 0.0) + jnp.log1p(jnp.exp(-jnp.abs(v)))
        dta3 = jnp.concatenate(_split3(dt * a), axis=1)
        la = jnp.where(is_fwd, cumulate(tri_lo, dta3), cumulate(tri_up, dta3))
        la_end = jnp.where(is_fwd, la[CHUNK - 1:CHUNK, :], la[0:1, :])
        wend = jnp.exp(la_end - la) * dt
        la2 = la * LOG2E
        nla = jnp.log2(jnp.maximum(dt, DT_FLOOR)) - la2
        lhs = jnp.concatenate(list(_split3(la2)) + [ones], axis=1)
        lhsp_ref[rows, :] = jnp.dot(lhs, scatter_ref[...],
                                    preferred_element_type=F32).astype(lhsp_ref.dtype)
        pieces = [p.astype(F32) for p in _split3(nla.T)] + [wend.T]
        for g in range(N_GROUPS):
            for j, val in enumerate(pieces):
                rowpk_ref[0, pl.ds(ROWS_PER_GROUP * g + HD * j, HD), rows] = val[HD * g:HD * (g + 1), :]


DT_CHUNKS_PER_STEP = 8


def _dt_prep(dt_raw, bias_p, alog_p, scatter, batch, seq):
    nc = seq // CHUNK
    per = min(DT_CHUNKS_PER_STEP, nc)
    steps = nc // per
    rows = per * CHUNK
    const = lambda b, c: (0, 0)
    return pl.pallas_call(
        _dt_kernel,
        grid=(batch, steps),
        in_specs=[pl.BlockSpec((rows, DT_LANES), lambda b, c: (b * steps + c, 0)),
                  pl.BlockSpec((1, DT_LANES), const),
                  pl.BlockSpec((1, DT_LANES), const),
                  pl.BlockSpec(scatter.shape, const)],
        out_specs=[pl.BlockSpec((rows, N_GROUPS * CHUNK), lambda b, c: (b * steps + c, 0)),
                   pl.BlockSpec((1, N_GROUPS * ROWS_PER_GROUP, rows), lambda b, c: (b, 0, c))],
        out_shape=[jax.ShapeDtypeStruct((batch * seq, N_GROUPS * CHUNK), BF16),
                   jax.ShapeDtypeStruct((batch, N_GROUPS * ROWS_PER_GROUP, seq), F32)],
        compiler_params=pltpu.CompilerParams(dimension_semantics=("parallel", "parallel")),
        name="dt_prep",
    )(dt_raw, bias_p, alog_p, scatter)


def _chunk_start(c):
    return c * CHUNK if isinstance(c, int) else pl.multiple_of(c * CHUNK, CHUNK)


def _conv_silu_chunk(refs, c, nc, w, b):
    seq = refs[0].shape[0]
    r0 = _chunk_start(c)
    if isinstance(c, int):
        p0, n0 = max(r0 - HALO, 0), min(r0 + CHUNK, seq - HALO)
    else:
        p0 = pl.multiple_of(jnp.maximum(r0 - HALO, 0), HALO)
        n0 = pl.multiple_of(jnp.minimum(r0 + CHUNK, seq - HALO), HALO)

    def rows(start, size):
        return jnp.concatenate([ref[pl.ds(start, size), :] for ref in refs], axis=1).astype(F32)

    prev, main, nxt = rows(p0, HALO), rows(r0, CHUNK), rows(n0, HALO)
    prev = jnp.where(c > 0, prev, 0.0)
    nxt = jnp.where(c < nc - 1, nxt, 0.0)
    ext = jnp.concatenate([prev, main, nxt], axis=0)
    total = CHUNK + 2 * HALO
    xm2 = pltpu.roll(ext, 2, 0)[HALO:HALO + CHUNK]
    xm1 = pltpu.roll(ext, 1, 0)[HALO:HALO + CHUNK]
    xp1 = pltpu.roll(ext, total - 1, 0)[HALO:HALO + CHUNK]
    y = w[0:1] * xm2 + w[1:2] * xm1 + w[2:3] * main + w[3:4] * xp1 + b
    return _silu(y)


def _chunk_operands(lhsp_ref, rowpk_ref, c):
    r0 = _chunk_start(c)
    return lhsp_ref[pl.ds(r0, CHUNK), :], rowpk_ref[0, :, pl.ds(r0, CHUNK)]


def _head_decay(lt, expand_ref, direction=None):
    if direction is None:
        ex = expand_ref[...]
    else:
        ex = expand_ref[:, direction * GROUP_W:(direction + 1) * GROUP_W]
    return jnp.exp2(jnp.dot(lt, ex, preferred_element_type=F32))


def _xs_blockdiag(xs_bf):
    lane = lax.broadcasted_iota(jnp.int32, xs_bf.shape, 1)
    zero = jnp.zeros_like(xs_bf)
    return jnp.concatenate(
        [jnp.where((lane >= r * HEAD_DIM) & (lane < (r + 1) * HEAD_DIM), xs_bf, zero)
         for r in range(HPG)], axis=0)


def _bt_weighted(bt_bf, rp, direction):
    btf = bt_bf.astype(F32)
    base = ROW_WEND + direction * HPG
    return jnp.concatenate([(btf * rp[base + r:base + r + 1, :]).astype(BF16) for r in range(HPG)],
                           axis=1)


def _ctx_kernel(xs_ref, b_ref, lhsp_ref, rowpk_ref, cwx_ref, cbx_ref, cwb_ref, cbb_ref,
                expand_ref, o_ref):
    nc = xs_ref.shape[0] // CHUNK
    cw = jnp.concatenate([cwx_ref[...], cwb_ref[...]], axis=1)
    cb = jnp.concatenate([cbx_ref[...], cbb_ref[...]], axis=1)
    contrib, decay = [], []
    for c in range(nc):
        xb = _conv_silu_chunk((xs_ref, b_ref), c, nc, cw, cb)
        xs_bf = xb[:, :GROUP_W].astype(BF16)
        bt = xb[:, GROUP_W:].T.astype(BF16)
        lt, rp = _chunk_operands(lhsp_ref, rowpk_ref, c)
        lhs = jnp.concatenate([_bt_weighted(bt, rp, 0), _bt_weighted(bt, rp, 1)], axis=0)
        contrib.append(jnp.dot(lhs, _xs_blockdiag(xs_bf), preferred_element_type=F32))
        decay.append(_head_decay(lt, expand_ref))
    hf = jnp.zeros((D_STATE, GROUP_W), F32)
    for c in range(nc):
        hf = hf * decay[c][CHUNK - 1:CHUNK, :GROUP_W] + contrib[c][:D_STATE]
    hb = jnp.zeros((D_STATE, GROUP_W), F32)
    for c in reversed(range(nc)):
        hb = hb * decay[c][0:1, GROUP_W:] + contrib[c][D_STATE:]
    o_ref[0, 0, 0] = hf
    o_ref[0, 0, 1] = hb


def _ctx_states(proj_ctx, lhsp, rowpk, conv_w, conv_b, expand, batch, seq):
    bcol = X_END // D_STATE
    const = lambda b, g: (0, 0)
    return pl.pallas_call(
        _ctx_kernel,
        grid=(batch, N_GROUPS),
        in_specs=[pl.BlockSpec((seq, GROUP_W), lambda b, g: (b, g)),
                  pl.BlockSpec((seq, D_STATE), lambda b, g: (b, bcol + g)),
                  pl.BlockSpec((seq, CHUNK), lambda b, g: (b, g)),
                  pl.BlockSpec((1, ROWS_PER_GROUP, seq), lambda b, g: (b, g, 0)),
                  pl.BlockSpec((SSM_CONV, GROUP_W), lambda b, g: (0, g)),
                  pl.BlockSpec((1, GROUP_W), lambda b, g: (0, g)),
                  pl.BlockSpec((SSM_CONV, D_STATE), lambda b, g: (0, bcol + g)),
                  pl.BlockSpec((1, D_STATE), lambda b, g: (0, bcol + g)),
                  pl.BlockSpec(expand.shape, const)],
        out_specs=pl.BlockSpec((1, 1, 2, D_STATE, GROUP_W), lambda b, g: (b, g, 0, 0, 0)),
        out_shape=jax.ShapeDtypeStruct((batch, N_GROUPS, 2, D_STATE, GROUP_W), F32),
        compiler_params=pltpu.CompilerParams(dimension_semantics=("parallel", "parallel")),
        name="ctx_states",
    )(proj_ctx, proj_ctx, lhsp, rowpk, conv_w, conv_b, conv_w, conv_b, expand)


def _ssd_kernel(xs_ref, b_ref, c_ref, z_ref, lhsp_ref, rowpk_ref, cwx_ref, cbx_ref, cwb_ref,
                cbb_ref, cwc_ref, cbc_ref, dsk_ref, nw_ref, h0_ref, rest_ref, expand_ref, eye_ref,
                o_ref, cx_s, cbt_s, cc_s, stash_s, hf_s, hb_s):
    seq = xs_ref.shape[0]
    nc = seq // CHUNK
    cw = jnp.concatenate([cwx_ref[...], cwb_ref[...], cwc_ref[...]], axis=1)
    cb = jnp.concatenate([cbx_ref[...], cbb_ref[...], cbc_ref[...]], axis=1)
    hf_s[...] = h0_ref[0, 0, 0]
    hb_s[...] = h0_ref[0, 0, 1]

    def back_body(i, carry):
        c = nc - 1 - i
        r0 = _chunk_start(c)
        xbc = _conv_silu_chunk((xs_ref, b_ref, c_ref), c, nc, cw, cb)
        xs_bf = xbc[:, :GROUP_W].astype(BF16)
        bt = xbc[:, GROUP_W:GROUP_W + D_STATE].T.astype(BF16)
        cx_s[pl.ds(r0, CHUNK), :] = xs_bf
        cbt_s[pl.ds(r0, CHUNK), :] = bt
        cc_s[pl.ds(r0, CHUNK), :] = xbc[:, GROUP_W + D_STATE:].astype(BF16)
        lt, rp = _chunk_operands(lhsp_ref, rowpk_ref, c)
        hb = hb_s[...]
        stash_s[pl.ds(r0, CHUNK), :] = hb.astype(BF16)
        dec = _head_decay(lt, expand_ref, 1)[0:1, :]
        contrib = jnp.dot(_bt_weighted(bt, rp, 1), _xs_blockdiag(xs_bf), preferred_element_type=F32)
        hb_s[...] = hb * dec + contrib
        return carry

    lax.fori_loop(0, nc, back_body, 0, unroll=2)

    prow = lax.broadcasted_iota(jnp.int32, (ROWS_PER_GROUP, HD * CHUNK), 0)
    pcol = lax.broadcasted_iota(jnp.int32, (ROWS_PER_GROUP, HD * CHUNK), 1)
    own_block = (prow < ROW_WEND) & (prow % HD == pcol // CHUNK)
    dsk, nw = dsk_ref[...], nw_ref[...]

    def fwd_body(c, carry):
        r0 = _chunk_start(c)
        xs_bf = cx_s[pl.ds(r0, CHUNK), :]
        bt = cbt_s[pl.ds(r0, CHUNK), :]
        cc = cc_s[pl.ds(r0, CHUNK), :]
        lt, rp = _chunk_operands(lhsp_ref, rowpk_ref, c)
        scores = jnp.dot(cc, bt, preferred_element_type=F32)
        rows = jnp.concatenate([rp.astype(BF16)] * HD, axis=1)
        rows = jnp.where(own_block, rows, jnp.zeros_like(rows))
        seg = jnp.dot(jnp.concatenate([lt, eye_ref[...]], axis=1),
                      jnp.concatenate([rows, rest_ref[...]], axis=0), preferred_element_type=F32)
        decay_dt = jnp.exp2(seg)
        mix = jnp.concatenate(
            [(scores * (decay_dt[:, r * CHUNK:(r + 1) * CHUNK]
                        + decay_dt[:, (HPG + r) * CHUNK:(HPG + r + 1) * CHUNK])).astype(BF16)
             for r in range(HPG)], axis=1)
        lhs = jnp.concatenate([mix, _bt_weighted(bt, rp, 0)], axis=0)
        prod = jnp.dot(lhs, _xs_blockdiag(xs_bf), preferred_element_type=F32)
        hf = hf_s[...]
        e = _head_decay(lt, expand_ref)
        states = jnp.concatenate([hf.astype(BF16), stash_s[pl.ds(r0, CHUNK), :]], axis=1)
        carried = jnp.dot(cc, states, preferred_element_type=F32) * e
        y = dsk * xs_bf.astype(F32) + prod[:CHUNK] + carried[:, :GROUP_W] + carried[:, GROUP_W:]
        hf_s[...] = hf * e[CHUNK - 1:CHUNK, :GROUP_W] + prod[CHUNK:]
        y = y * z_ref[pl.ds(r0, CHUNK), :].astype(F32)
        ms = jnp.mean(y * y, axis=-1, keepdims=True)
        o_ref[pl.ds(r0, CHUNK), :] = (y * lax.rsqrt(ms + EPS) * nw).astype(o_ref.dtype)
        return carry

    lax.fori_loop(0, nc, fwd_body, 0, unroll=4)


def _ssd(proj, lhsp, rowpk, conv_w, conv_b, dskip_e, norm_w, h0, rest, expand, eye, batch, seq):
    bcol = X_END // D_STATE
    ccol = B_END // D_STATE
    zcol = P_Z // GROUP_W
    gmap = lambda b, g: (0, g)
    const = lambda b, g: (0, 0)
    return pl.pallas_call(
        _ssd_kernel,
        grid=(batch, N_GROUPS),
        in_specs=[pl.BlockSpec((seq, GROUP_W), lambda b, g: (b, g)),
                  pl.BlockSpec((seq, D_STATE), lambda b, g: (b, bcol + g)),
                  pl.BlockSpec((seq, D_STATE), lambda b, g: (b, ccol + g)),
                  pl.BlockSpec((seq, GROUP_W), lambda b, g: (b, zcol + g)),
                  pl.BlockSpec((seq, CHUNK), lambda b, g: (b, g)),
                  pl.BlockSpec((1, ROWS_PER_GROUP, seq), lambda b, g: (b, g, 0)),
                  pl.BlockSpec((SSM_CONV, GROUP_W), gmap),
                  pl.BlockSpec((1, GROUP_W), gmap),
                  pl.BlockSpec((SSM_CONV, D_STATE), lambda b, g: (0, bcol + g)),
                  pl.BlockSpec((1, D_STATE), lambda b, g: (0, bcol + g)),
                  pl.BlockSpec((SSM_CONV, D_STATE), lambda b, g: (0, ccol + g)),
                  pl.BlockSpec((1, D_STATE), lambda b, g: (0, ccol + g)),
                  pl.BlockSpec((1, GROUP_W), gmap),
                  pl.BlockSpec((1, GROUP_W), gmap),
                  pl.BlockSpec((1, 1, 2, D_STATE, GROUP_W), lambda b, g: (b, g, 0, 0, 0)),
                  pl.BlockSpec(rest.shape, const),
                  pl.BlockSpec(expand.shape, const),
                  pl.BlockSpec(eye.shape, const)],
        out_specs=pl.BlockSpec((seq, GROUP_W), lambda b, g: (b, g)),
        out_shape=jax.ShapeDtypeStruct((batch * seq, D_INNER), BF16),
        scratch_shapes=[pltpu.VMEM((seq, GROUP_W), BF16),
                        pltpu.VMEM((seq, D_STATE), BF16),
                        pltpu.VMEM((seq, D_STATE), BF16),
                        pltpu.VMEM((seq, GROUP_W), BF16),
                        pltpu.VMEM((D_STATE, GROUP_W), F32),
                        pltpu.VMEM((D_STATE, GROUP_W), F32)],
        compiler_params=pltpu.CompilerParams(
            dimension_semantics=("parallel", "parallel"), vmem_limit_bytes=VMEM_LIMIT_BYTES),
        name="ssd",
    )(proj, proj, proj, proj, lhsp, rowpk, conv_w, conv_b, conv_w, conv_b, conv_w, conv_b,
      dskip_e, norm_w, h0, rest, expand, eye)


CONF_LANE_BLOCK = 128


N_TAIL_PROJ = 5


def _tail_kernel(yn_ref, x_ref, xnext_ref, mod_ref, modnext_ref, nw_ref, wt_ref, cw_ref, cb_ref,
                 lnw_ref, lnb_ref, wos_ref, woc_ref, wo_ref, fnw_ref, o_ref,
                 upad_s, uc_s, proj_s, act_s, hnext_s):
    tm = x_ref.shape[0]
    nseq = tm // GRID_W

    def project(xv, mref):
        h = _modulated_norm(xv, nw_ref[...], mref).astype(BF16)
        for t in range(N_TAIL_PROJ):
            cols = slice(t * D_CONF, (t + 1) * D_CONF)
            proj_s[:, cols] = jnp.dot(h, wt_ref[:, cols], preferred_element_type=F32)

    def proj(t):
        return proj_s[:, t * D_CONF:(t + 1) * D_CONF]

    @pl.when(pl.program_id(0) == 0)
    def _():
        upad_s[...] = jnp.zeros_like(upad_s)
        project(x_ref[...], mod_ref)

    u = proj(0) * _sigmoid(proj(1))
    for q in range(nseq):
        upad_s[pl.ds(q * CONF_SLOT + CONF_PAD_ROWS, GRID_W), :] = u[q * GRID_W:(q + 1) * GRID_W]
    act_s[0] = _silu(proj(2))
    act_s[1] = _sigmoid(proj(3))
    act_s[2] = _sigmoid(proj(4))
    hnext_s[...] = _modulated_norm(xnext_ref[...], nw_ref[...], modnext_ref).astype(BF16)

    def project_next(t):
        cols = slice(t * D_CONF, (t + 1) * D_CONF)
        proj_s[:, cols] = jnp.dot(hnext_s[...], wt_ref[:, cols], preferred_element_type=F32)

    x = x_ref[...]
    first_tap = CONF_PAD_ROWS - CONF_KERNEL // 2
    for q in range(nseq):
        for t in range(q * N_TAIL_PROJ // nseq, (q + 1) * N_TAIL_PROJ // nseq):
            project_next(t)
        for j in range(D_CONF // CONF_LANE_BLOCK):
            lanes = pl.ds(j * CONF_LANE_BLOCK, CONF_LANE_BLOCK)
            acc = jnp.broadcast_to(cb_ref[:, lanes], (GRID_W, CONF_LANE_BLOCK))
            padded = upad_s[pl.ds(q * CONF_SLOT, CONF_SLOT), lanes]
            for r in range(SUBLANES):
                shifted = padded if r == 0 else pltpu.roll(padded, CONF_SLOT - r, 0)
                for a in range((CONF_SLOT - GRID_W) // SUBLANES):
                    k = SUBLANES * a + r - first_tap
                    if 0 <= k < CONF_KERNEL:
                        acc = acc + shifted[SUBLANES * a:SUBLANES * a + GRID_W] * cw_ref[k:k + 1, lanes]
            uc_s[pl.ds(q * GRID_W, GRID_W), lanes] = acc
    uc = uc_s[...]
    mu = jnp.mean(uc, axis=-1, keepdims=True)
    dev = uc - mu
    var = jnp.mean(dev * dev, axis=-1, keepdims=True)
    ln = dev * lax.rsqrt(var + EPS) * lnw_ref[...] + lnb_ref[...]
    u2 = _silu(ln) * act_s[0]
    branch_conf = jnp.dot(u2.astype(BF16), woc_ref[...], preferred_element_type=F32)
    branch_ssm = jnp.dot(yn_ref[...], wos_ref[...], preferred_element_type=F32)
    merged = act_s[1] * branch_ssm + act_s[2] * branch_conf
    out = jnp.dot(merged.astype(BF16), wo_ref[...], preferred_element_type=F32)
    xn = x + mod_ref[0, 2:3, :] * out
    ms = jnp.mean(xn * xn, axis=-1, keepdims=True)
    o_ref[...] = xn * lax.rsqrt(ms + EPS) * fnw_ref[...]


def _tail(yn, x2d, mod3, norm_w, w_tail, conf_w, conf_b, ln_w, ln_b, w_os, w_oc, w_o, fn_w, *,
          tm, seq):
    m, d = x2d.shape
    tiles_per_batch = seq // tm
    const = lambda i: (0, 0)
    n_tiles = m // tm
    nxt = lambda i: jnp.minimum(i + 1, n_tiles - 1)
    once = dict(pipeline_mode=pl.Buffered(1))
    return pl.pallas_call(
        _tail_kernel,
        grid=(m // tm,),
        in_specs=[pl.BlockSpec((tm, D_INNER), lambda i: (i, 0)),
                  pl.BlockSpec((tm, d), lambda i: (i, 0)),
                  pl.BlockSpec((tm, d), lambda i: (nxt(i), 0)),
                  pl.BlockSpec((1, 3, d), lambda i: (i // tiles_per_batch, 0, 0)),
                  pl.BlockSpec((1, 3, d), lambda i: (nxt(i) // tiles_per_batch, 0, 0)),
                  pl.BlockSpec((1, d), const),
                  pl.BlockSpec(w_tail.shape, const, **once),
                  pl.BlockSpec((CONF_KERNEL, D_CONF), const),
                  pl.BlockSpec((1, D_CONF), const),
                  pl.BlockSpec((1, D_CONF), const),
                  pl.BlockSpec((1, D_CONF), const),
                  pl.BlockSpec((D_INNER, D_MODEL), const, **once),
                  pl.BlockSpec((D_CONF, D_MODEL), const, **once),
                  pl.BlockSpec((D_MODEL, D_MODEL), const, **once),
                  pl.BlockSpec((1, D_MODEL), const)],
        out_specs=pl.BlockSpec((tm, d), lambda i: (i, 0)),
        out_shape=jax.ShapeDtypeStruct((m, d), F32),
        scratch_shapes=[pltpu.VMEM((tm // GRID_W * CONF_SLOT, D_CONF), F32),
                        pltpu.VMEM((tm, D_CONF), F32),
                        pltpu.VMEM((tm, N_TAIL_PROJ * D_CONF), F32),
                        pltpu.VMEM((3, tm, D_CONF), F32),
                        pltpu.VMEM((tm, D_MODEL), BF16)],
        compiler_params=pltpu.CompilerParams(
            dimension_semantics=("arbitrary",), vmem_limit_bytes=VMEM_LIMIT_BYTES),
        name="tail",
    )(yn, x2d, x2d, mod3, mod3, norm_w, w_tail, conf_w, conf_b, ln_w, ln_b, w_os, w_oc, w_o, fn_w)


def _head_perm():
    return np.array([d * N_HEADS + g * HPG + r
                     for g in range(N_GROUPS) for d in range(2) for r in range(HPG)])


def kernel(x, c, ctx, c_ctx, w_mod, b_mod, norm_w, w_in, ssm_conv_w, ssm_conv_b, dt_bias, a_log,
           d_skip, ssm_norm_w, w_out_ssm, conf_conv_w, conf_conv_b, conf_ln_w, conf_ln_b,
           w_out_conf, w_out, final_norm_w):
    batch, seq, d = x.shape
    ctx_len = ctx.shape[1]
    assert w_in.shape[0] == 1, "single trunk layer"
    assert d == D_MODEL and seq % CHUNK == 0 and ctx_len % CHUNK == 0 and batch + 1 <= 8

    w_in0 = w_in[0]
    w_main = jnp.concatenate([w_in0[:, :C_END], w_in0[:, DT_END:Z_END]], axis=1).astype(BF16)
    w_tail = w_in0[:, Z_END:].astype(BF16)
    perm = _head_perm()
    pad_heads = DT_LANES - 2 * N_HEADS
    w_dt = jnp.pad(w_in0[:, C_END:DT_END][:, perm], ((0, 0), (0, pad_heads))).astype(BF16)
    bias_p = jnp.pad(dt_bias[0].reshape(-1)[perm], (0, pad_heads)).reshape(1, DT_LANES)
    alog_p = jnp.pad(a_log[0].reshape(-1)[perm], (0, pad_heads)).reshape(1, DT_LANES)
    conv_w = ssm_conv_w[0]
    conv_b = ssm_conv_b[0].reshape(1, C_END)
    dskip_e = jnp.repeat(d_skip[0], HEAD_DIM).reshape(1, D_INNER)
    nw2 = norm_w[0].reshape(1, d)
    rest, expand, eye, scatter = _decay_constants()

    c_all = jnp.concatenate([c, c_ctx[None, :], jnp.zeros((8 - batch - 1, d), F32)], axis=0)
    mod3 = _mod(c_all, w_mod[0], b_mod[0]).reshape(8, 3, d)

    ctx_rows = batch * ctx_len
    tm_ctx = min(1024, ctx_rows)
    proj_ctx, dtraw_ctx = _in_proj(ctx.reshape(ctx_rows, d), mod3, nw2, w_main, w_dt,
                                   tm=tm_ctx, tn=1024, n_col_tiles=B_END // 1024,
                                   n_plain_tiles=B_END // 1024, mod_row=lambda i: batch)
    lhsp_ctx, rowpk_ctx = _dt_prep(dtraw_ctx, bias_p, alog_p, scatter, batch, ctx_len)
    h0 = _ctx_states(proj_ctx, lhsp_ctx, rowpk_ctx, conv_w, conv_b, expand, batch, ctx_len)

    m = batch * seq
    x2d = x.reshape(m, d)
    tm = min(1024, seq)
    tiles_per_batch = seq // tm
    proj, dtraw = _in_proj(x2d, mod3, nw2, w_main, w_dt, tm=tm, tn=IN_PROJ_TN,
                           n_col_tiles=P_COLS // IN_PROJ_TN, n_plain_tiles=C_END // IN_PROJ_TN,
                           mod_row=lambda i: i // tiles_per_batch)
    lhsp, rowpk = _dt_prep(dtraw, bias_p, alog_p, scatter, batch, seq)
    yn = _ssd(proj, lhsp, rowpk, conv_w, conv_b, dskip_e, ssm_norm_w[0].reshape(1, D_INNER), h0,
              rest, expand, eye, batch, seq)
    out = _tail(yn, x2d, mod3, nw2, w_tail, conf_conv_w[0], conf_conv_b[0].reshape(1, D_CONF),
                conf_ln_w[0].reshape(1, D_CONF), conf_ln_b[0].reshape(1, D_CONF),
                w_out_ssm[0].astype(BF16), w_out_conf[0].astype(BF16), w_out[0].astype(BF16),
                final_norm_w.reshape(1, d), tm=min(256, seq), seq=seq)
    return out.reshape(batch, seq, d)
```

```python
import functools

import numpy as np
import jax
import jax.numpy as jnp
from jax import lax
from jax.experimental import pallas as pl
from jax.experimental.pallas import tpu as pltpu

F32 = jnp.float32
BF16 = jnp.bfloat16

D_MODEL = 1024
GRID_W = 64
D_INNER = 2 * D_MODEL
HEAD_DIM = 64
N_HEADS = D_INNER // HEAD_DIM
N_GROUPS = 8
HPG = N_HEADS // N_GROUPS
D_STATE = 128
SSM_CONV = 4
CHUNK = 128
D_CONF = D_MODEL
CONF_KERNEL = 31
EPS = 1e-6

GN = N_GROUPS * D_STATE
X_END = D_INNER
B_END = X_END + GN
C_END = B_END + GN
DT_END = C_END + 2 * N_HEADS
Z_END = DT_END + D_INNER
GLU_END = Z_END + 2 * D_CONF
CG_END = GLU_END + D_CONF
IN_COLS = CG_END + 2 * D_MODEL

GROUP_W = HPG * HEAD_DIM
DT_LANES = 128
P_Z = C_END
P_COLS = C_END + D_INNER
IN_PROJ_TN = 1024
IN_PROJ_TM = 512

SUBLANES = 8
HALO = 16
CONF_PAD_ROWS = 16
CONF_SLOT = GRID_W + 2 * CONF_PAD_ROWS
VMEM_LIMIT_BYTES = 56 * 1024 * 1024


def _sigmoid(v):
    return 1.0 / (1.0 + jnp.exp(-v))


def _silu(v):
    return v * _sigmoid(v)


def _mod_kernel(c_ref, w_ref, b_ref, o_ref):
    s = _silu(c_ref[...])
    o_ref[...] = jnp.dot(s, w_ref[...], preferred_element_type=F32,
                         precision=lax.Precision.HIGHEST) + b_ref[...]


def _mod(c_all, w_mod, b_mod):
    rows, d = c_all.shape
    n = w_mod.shape[1]
    tn = 1024
    return pl.pallas_call(
        _mod_kernel,
        grid=(n // tn,),
        in_specs=[pl.BlockSpec((rows, d), lambda j: (0, 0)),
                  pl.BlockSpec((d, tn), lambda j: (0, j)),
                  pl.BlockSpec((1, tn), lambda j: (0, j))],
        out_specs=pl.BlockSpec((rows, tn), lambda j: (0, j)),
        out_shape=jax.ShapeDtypeStruct((rows, n), F32),
        name="mod",
    )(c_all, w_mod, b_mod.reshape(1, n))


def _modulated_norm(x, nw, mod_ref):
    ms = jnp.mean(x * x, axis=-1, keepdims=True)
    y = x * lax.rsqrt(ms + EPS) * nw
    return y * (1.0 + mod_ref[0, 1:2, :]) + mod_ref[0, 0:1, :]


def _inproj_kernel(x_ref, mod_ref, nw_ref, w_ref, wdt_ref, o_ref, dt_ref, *, plain_cols, tn):
    hb = _modulated_norm(x_ref[...], nw_ref[...], mod_ref).astype(BF16)
    dt_ref[...] = jnp.dot(hb, wdt_ref[...], preferred_element_type=F32)
    for c0 in range(0, w_ref.shape[1], tn):
        acc = jnp.dot(hb, w_ref[:, c0:c0 + tn], preferred_element_type=F32)
        o_ref[:, c0:c0 + tn] = (acc if c0 < plain_cols else _silu(acc)).astype(BF16)


def _in_proj(x2d, mod3, norm_w, w_main, w_dt, *, tm, n_cols, plain_cols, mod_row):
    m, d = x2d.shape
    const = lambda i: (0, 0)
    once = dict(pipeline_mode=pl.Buffered(1))
    return pl.pallas_call(
        functools.partial(_inproj_kernel, plain_cols=plain_cols, tn=IN_PROJ_TN),
        grid=(m // tm,),
        in_specs=[pl.BlockSpec((tm, d), lambda i: (i, 0)),
                  pl.BlockSpec((1, 3, d), lambda i: (mod_row(i), 0, 0)),
                  pl.BlockSpec((1, d), const),
                  pl.BlockSpec((d, n_cols), const, **once),
                  pl.BlockSpec((d, DT_LANES), const, **once)],
        out_specs=[pl.BlockSpec((tm, n_cols), lambda i: (i, 0)),
                   pl.BlockSpec((tm, DT_LANES), lambda i: (i, 0))],
        out_shape=[jax.ShapeDtypeStruct((m, n_cols), BF16),
                   jax.ShapeDtypeStruct((m, DT_LANES), F32)],
        compiler_params=pltpu.CompilerParams(
            dimension_semantics=("parallel",), vmem_limit_bytes=VMEM_LIMIT_BYTES),
        name="in_proj",
    )(x2d, mod3, norm_w, w_main, w_dt)


N_PIECES = 3
HD = 2 * HPG
LHS_ONES = 0
LHS_LA = 32
ROW_WEND = N_PIECES * HD
ROWS_PER_GROUP = ROW_WEND + HD
MASK_BIG = 1e30
DT_FLOOR = 1e-37
LOG2E = 1.4426950408889634


def _decay_constants():
    k = np.arange(CHUNK)[:, None]
    s = np.arange(CHUNK)[None, :]
    rest = np.zeros((2 * CHUNK, HD * CHUNK), np.float32)
    expand = np.zeros((CHUNK, 2 * GROUP_W), np.float32)
    for h in range(HD):
        cols = slice(h * CHUNK, (h + 1) * CHUNK)
        hidden = (s > k) if h < HPG else (s < k)
        rest[CHUNK:, cols] = np.where(hidden, -MASK_BIG, 0.0)
        for j in range(N_PIECES):
            rest[LHS_LA + HD * j + h, cols] = 1.0
            expand[LHS_LA + HD * j + h, h * HEAD_DIM:(h + 1) * HEAD_DIM] = 1.0
    scatter = np.zeros((4 * DT_LANES, N_GROUPS * CHUNK), np.float32)
    for g in range(N_GROUPS):
        scatter[N_PIECES * DT_LANES, g * CHUNK + LHS_ONES:g * CHUNK + LHS_ONES + ROW_WEND] = 1.0
        for h in range(HD):
            for j in range(N_PIECES):
                scatter[j * DT_LANES + HD * g + h, g * CHUNK + LHS_LA + HD * j + h] = 1.0
    return (jnp.asarray(rest[ROWS_PER_GROUP:], BF16), jnp.asarray(expand, BF16),
            jnp.asarray(np.eye(CHUNK), BF16), jnp.asarray(scatter, BF16))


def _split3(v):
    p0 = v.astype(BF16)
    r1 = v - p0.astype(F32)
    p1 = r1.astype(BF16)
    p2 = (r1 - p1.astype(F32)).astype(BF16)
    return p0, p1, p2


def _dt_kernel(raw_ref, bias_ref, alog_ref, scatter_ref, lhsp_ref, rowpk_ref):
    n_chunks = raw_ref.shape[0] // CHUNK
    bias = bias_ref[...]
    a = -jnp.exp(alog_ref[...])
    row = lax.broadcasted_iota(jnp.int32, (CHUNK, CHUNK), 0)
    col = lax.broadcasted_iota(jnp.int32, (CHUNK, CHUNK), 1)
    tri_lo = (col <= row).astype(BF16)
    tri_up = (col >= row).astype(BF16)
    lane = lax.broadcasted_iota(jnp.int32, (1, DT_LANES), 1)
    is_fwd = (lane % HD) < HPG
    ones = jnp.ones((CHUNK, DT_LANES), BF16)

    def cumulate(tri, pieces):
        acc = jnp.dot(tri, pieces, preferred_element_type=F32)
        return acc[:, :DT_LANES] + acc[:, DT_LANES:2 * DT_LANES] + acc[:, 2 * DT_LANES:]

    for ci in range(n_chunks):
        rows = pl.ds(ci * CHUNK, CHUNK)
        v = raw_ref[rows, :] + bias
        dt = jnp.maximum(v, 0.0) + jnp.log1p(jnp.exp(-jnp.abs(v)))
        dta3 = jnp.concatenate(_split3(dt * a), axis=1)
        la = jnp.where(is_fwd, cumulate(tri_lo, dta3), cumulate(tri_up, dta3))
        la_end = jnp.where(is_fwd, la[CHUNK - 1:CHUNK, :], la[0:1, :])
        wend = jnp.exp(la_end - la) * dt
        la2 = la * LOG2E
        nla = jnp.log2(jnp.maximum(dt, DT_FLOOR)) - la2
        lhs = jnp.concatenate(list(_split3(la2)) + [ones], axis=1)
        lhsp_ref[rows, :] = jnp.dot(lhs, scatter_ref[...],
                                    preferred_element_type=F32).astype(lhsp_ref.dtype)
        pieces = [p.astype(F32) for p in _split3(nla.T)] + [wend.T]
        for g in range(N_GROUPS):
            for j, val in enumerate(pieces):
                rowpk_ref[0, pl.ds(ROWS_PER_GROUP * g + HD * j, HD), rows] = val[HD * g:HD * (g + 1), :]


DT_CHUNKS_PER_STEP = 8


def _dt_prep(dt_raw, bias_p, alog_p, scatter, batch, seq):
    nc = seq // CHUNK
    per = min(DT_CHUNKS_PER_STEP, nc)
    steps = nc // per
    rows = per * CHUNK
    const = lambda b, c: (0, 0)
    return pl.pallas_call(
        _dt_kernel,
        grid=(batch, steps),
        in_specs=[pl.BlockSpec((rows, DT_LANES), lambda b, c: (b * steps + c, 0)),
                  pl.BlockSpec((1, DT_LANES), const),
                  pl.BlockSpec((1, DT_LANES), const),
                  pl.BlockSpec(scatter.shape, const)],
        out_specs=[pl.BlockSpec((rows, N_GROUPS * CHUNK), lambda b, c: (b * steps + c, 0)),
                   pl.BlockSpec((1, N_GROUPS * ROWS_PER_GROUP, rows), lambda b, c: (b, 0, c))],
        out_shape=[jax.ShapeDtypeStruct((batch * seq, N_GROUPS * CHUNK), BF16),
                   jax.ShapeDtypeStruct((batch, N_GROUPS * ROWS_PER_GROUP, seq), F32)],
        compiler_params=pltpu.CompilerParams(dimension_semantics=("parallel", "parallel")),
        name="dt_prep",
    )(dt_raw, bias_p, alog_p, scatter)


def _chunk_start(c):
    return c * CHUNK if isinstance(c, int) else pl.multiple_of(c * CHUNK, CHUNK)


def _conv_silu_chunk(refs, c, nc, w, b):
    seq = refs[0].shape[0]
    r0 = _chunk_start(c)
    if isinstance(c, int):
        p0, n0 = max(r0 - HALO, 0), min(r0 + CHUNK, seq - HALO)
    else:
        p0 = pl.multiple_of(jnp.maximum(r0 - HALO, 0), HALO)
        n0 = pl.multiple_of(jnp.minimum(r0 + CHUNK, seq - HALO), HALO)

    def rows(start, size):
        return jnp.concatenate([ref[pl.ds(start, size), :] for ref in refs], axis=1).astype(F32)

    prev, main, nxt = rows(p0, HALO), rows(r0, CHUNK), rows(n0, HALO)
    prev = jnp.where(c > 0, prev, 0.0)
    nxt = jnp.where(c < nc - 1, nxt, 0.0)
    ext = jnp.concatenate([prev, main, nxt], axis=0)
    total = CHUNK + 2 * HALO
    xm2 = pltpu.roll(ext, 2, 0)[HALO:HALO + CHUNK]
    xm1 = pltpu.roll(ext, 1, 0)[HALO:HALO + CHUNK]
    xp1 = pltpu.roll(ext, total - 1, 0)[HALO:HALO + CHUNK]
    y = w[0:1] * xm2 + w[1:2] * xm1 + w[2:3] * main + w[3:4] * xp1 + b
    return _silu(y)


def _chunk_operands(lhsp_ref, rowpk_ref, c):
    r0 = _chunk_start(c)
    return lhsp_ref[pl.ds(r0, CHUNK), :], rowpk_ref[0, :, pl.ds(r0, CHUNK)]


def _head_decay(lt, expand_ref, direction=None):
    if direction is None:
        ex = expand_ref[...]
    else:
        ex = expand_ref[:, direction * GROUP_W:(direction + 1) * GROUP_W]
    return jnp.exp2(jnp.dot(lt, ex, preferred_element_type=F32))


def _xs_blockdiag(xs_bf):
    lane = lax.broadcasted_iota(jnp.int32, xs_bf.shape, 1)
    zero = jnp.zeros_like(xs_bf)
    return jnp.concatenate(
        [jnp.where((lane >= r * HEAD_DIM) & (lane < (r + 1) * HEAD_DIM), xs_bf, zero)
         for r in range(HPG)], axis=0)


def _bt_weighted(bt_bf, rp, direction):
    base = ROW_WEND + direction * HPG
    w = rp[base:base + HPG, :].astype(BF16)
    return jnp.concatenate([bt_bf * w[r:r + 1, :] for r in range(HPG)], axis=1)


def _ctx_kernel(xs_ref, b_ref, lhsp_ref, rowpk_ref, cwx_ref, cbx_ref, cwb_ref, cbb_ref,
                expand_ref, o_ref):
    nc = xs_ref.shape[0] // CHUNK
    cw = jnp.concatenate([cwx_ref[...], cwb_ref[...]], axis=1)
    cb = jnp.concatenate([cbx_ref[...], cbb_ref[...]], axis=1)
    contrib, decay = [], []
    for c in range(nc):
        xb = _conv_silu_chunk((xs_ref, b_ref), c, nc, cw, cb)
        xs_bf = xb[:, :GROUP_W].astype(BF16)
        bt = xb[:, GROUP_W:].T.astype(BF16)
        lt, rp = _chunk_operands(lhsp_ref, rowpk_ref, c)
        lhs = jnp.concatenate([_bt_weighted(bt, rp, 0), _bt_weighted(bt, rp, 1)], axis=0)
        contrib.append(jnp.dot(lhs, _xs_blockdiag(xs_bf), preferred_element_type=F32))
        decay.append(_head_decay(lt, expand_ref))
    hf = jnp.zeros((D_STATE, GROUP_W), F32)
    for c in range(nc):
        hf = hf * decay[c][CHUNK - 1:CHUNK, :GROUP_W] + contrib[c][:D_STATE]
    hb = jnp.zeros((D_STATE, GROUP_W), F32)
    for c in reversed(range(nc)):
        hb = hb * decay[c][0:1, GROUP_W:] + contrib[c][D_STATE:]
    o_ref[0, 0, 0] = hf
    o_ref[0, 0, 1] = hb


def _ctx_states(proj_ctx, lhsp, rowpk, conv_w, conv_b, expand, batch, seq):
    bcol = X_END // D_STATE
    const = lambda b, g: (0, 0)
    return pl.pallas_call(
        _ctx_kernel,
        grid=(batch, N_GROUPS),
        in_specs=[pl.BlockSpec((seq, GROUP_W), lambda b, g: (b, g)),
                  pl.BlockSpec((seq, D_STATE), lambda b, g: (b, bcol + g)),
                  pl.BlockSpec((seq, CHUNK), lambda b, g: (b, g)),
                  pl.BlockSpec((1, ROWS_PER_GROUP, seq), lambda b, g: (b, g, 0)),
                  pl.BlockSpec((SSM_CONV, GROUP_W), lambda b, g: (0, g)),
                  pl.BlockSpec((1, GROUP_W), lambda b, g: (0, g)),
                  pl.BlockSpec((SSM_CONV, D_STATE), lambda b, g: (0, bcol + g)),
                  pl.BlockSpec((1, D_STATE), lambda b, g: (0, bcol + g)),
                  pl.BlockSpec(expand.shape, const)],
        out_specs=pl.BlockSpec((1, 1, 2, D_STATE, GROUP_W), lambda b, g: (b, g, 0, 0, 0)),
        out_shape=jax.ShapeDtypeStruct((batch, N_GROUPS, 2, D_STATE, GROUP_W), F32),
        compiler_params=pltpu.CompilerParams(dimension_semantics=("parallel", "parallel")),
        name="ctx_states",
    )(proj_ctx, proj_ctx, lhsp, rowpk, conv_w, conv_b, conv_w, conv_b, expand)


def _ssd_kernel(xs_ref, b_ref, c_ref, z_ref, lhsp_ref, rowpk_ref, cwx_ref, cbx_ref, cwb_ref,
                cbb_ref, cwc_ref, cbc_ref, dsk_ref, nw_ref, h0_ref, rest_ref, expand_ref, eye_ref,
                o_ref, cx_s, cbt_s, cc_s, stash_s, hf_s, hb_s):
    seq = xs_ref.shape[0]
    nc = seq // CHUNK
    cw = jnp.concatenate([cwx_ref[...], cwb_ref[...], cwc_ref[...]], axis=1)
    cb = jnp.concatenate([cbx_ref[...], cbb_ref[...], cbc_ref[...]], axis=1)
    hf_s[...] = h0_ref[0, 0, 0]
    hb_s[...] = h0_ref[0, 0, 1]

    def back_body(i, carry):
        c = nc - 1 - i
        r0 = _chunk_start(c)
        xbc = _conv_silu_chunk((xs_ref, b_ref, c_ref), c, nc, cw, cb)
        xs_bf = xbc[:, :GROUP_W].astype(BF16)
        bt = xbc[:, GROUP_W:GROUP_W + D_STATE].T.astype(BF16)
        cx_s[pl.ds(r0, CHUNK), :] = xs_bf
        cbt_s[pl.ds(r0, CHUNK), :] = bt
        cc_s[pl.ds(r0, CHUNK), :] = xbc[:, GROUP_W + D_STATE:].astype(BF16)
        lt, rp = _chunk_operands(lhsp_ref, rowpk_ref, c)
        hb = hb_s[...]
        stash_s[pl.ds(r0, CHUNK), :] = hb.astype(BF16)
        dec = _head_decay(lt, expand_ref, 1)[0:1, :]
        contrib = jnp.dot(_bt_weighted(bt, rp, 1), _xs_blockdiag(xs_bf), preferred_element_type=F32)
        hb_s[...] = hb * dec + contrib
        return carry

    lax.fori_loop(0, nc, back_body, 0, unroll=4)

    prow = lax.broadcasted_iota(jnp.int32, (ROWS_PER_GROUP, HD * CHUNK), 0)
    pcol = lax.broadcasted_iota(jnp.int32, (ROWS_PER_GROUP, HD * CHUNK), 1)
    own_block = (prow < ROW_WEND) & (prow % HD == pcol // CHUNK)
    dsk, nw = dsk_ref[...], nw_ref[...]

    def fwd_body(c, carry):
        r0 = _chunk_start(c)
        xs_bf = cx_s[pl.ds(r0, CHUNK), :]
        bt = cbt_s[pl.ds(r0, CHUNK), :]
        cc = cc_s[pl.ds(r0, CHUNK), :]
        lt, rp = _chunk_operands(lhsp_ref, rowpk_ref, c)
        scores = jnp.dot(cc, bt, preferred_element_type=F32)
        rows = jnp.concatenate([rp.astype(BF16)] * HD, axis=1)
        rows = jnp.where(own_block, rows, jnp.zeros_like(rows))
        seg = jnp.dot(jnp.concatenate([lt, eye_ref[...]], axis=1),
                      jnp.concatenate([rows, rest_ref[...]], axis=0), preferred_element_type=F32)
        decay_dt = jnp.exp2(seg)
        mix = jnp.concatenate(
            [(scores * (decay_dt[:, r * CHUNK:(r + 1) * CHUNK]
                        + decay_dt[:, (HPG + r) * CHUNK:(HPG + r + 1) * CHUNK])).astype(BF16)
             for r in range(HPG)], axis=1)
        lhs = jnp.concatenate([mix, _bt_weighted(bt, rp, 0)], axis=0)
        prod = jnp.dot(lhs, _xs_blockdiag(xs_bf), preferred_element_type=F32)
        hf = hf_s[...]
        e = _head_decay(lt, expand_ref)
        states = jnp.concatenate([hf.astype(BF16), stash_s[pl.ds(r0, CHUNK), :]], axis=1)
        carried = jnp.dot(cc, states, preferred_element_type=F32) * e
        y = dsk * xs_bf.astype(F32) + prod[:CHUNK] + carried[:, :GROUP_W] + carried[:, GROUP_W:]
        hf_s[...] = hf * e[CHUNK - 1:CHUNK, :GROUP_W] + prod[CHUNK:]
        y = y * z_ref[pl.ds(r0, CHUNK), :].astype(F32)
        ms = jnp.mean(y * y, axis=-1, keepdims=True)
        o_ref[pl.ds(r0, CHUNK), :] = (y * lax.rsqrt(ms + EPS) * nw).astype(o_ref.dtype)
        return carry

    lax.fori_loop(0, nc, fwd_body, 0, unroll=4)


def _ssd(proj, lhsp, rowpk, conv_w, conv_b, dskip_e, norm_w, h0, rest, expand, eye, batch, seq):
    bcol = X_END // D_STATE
    ccol = B_END // D_STATE
    zcol = P_Z // GROUP_W
    gmap = lambda b, g: (0, g)
    const = lambda b, g: (0, 0)
    return pl.pallas_call(
        _ssd_kernel,
        grid=(batch, N_GROUPS),
        in_specs=[pl.BlockSpec((seq, GROUP_W), lambda b, g: (b, g)),
                  pl.BlockSpec((seq, D_STATE), lambda b, g: (b, bcol + g)),
                  pl.BlockSpec((seq, D_STATE), lambda b, g: (b, ccol + g)),
                  pl.BlockSpec((seq, GROUP_W), lambda b, g: (b, zcol + g)),
                  pl.BlockSpec((seq, CHUNK), lambda b, g: (b, g)),
                  pl.BlockSpec((1, ROWS_PER_GROUP, seq), lambda b, g: (b, g, 0)),
                  pl.BlockSpec((SSM_CONV, GROUP_W), gmap),
                  pl.BlockSpec((1, GROUP_W), gmap),
                  pl.BlockSpec((SSM_CONV, D_STATE), lambda b, g: (0, bcol + g)),
                  pl.BlockSpec((1, D_STATE), lambda b, g: (0, bcol + g)),
                  pl.BlockSpec((SSM_CONV, D_STATE), lambda b, g: (0, ccol + g)),
                  pl.BlockSpec((1, D_STATE), lambda b, g: (0, ccol + g)),
                  pl.BlockSpec((1, GROUP_W), gmap),
                  pl.BlockSpec((1, GROUP_W), gmap),
                  pl.BlockSpec((1, 1, 2, D_STATE, GROUP_W), lambda b, g: (b, g, 0, 0, 0)),
                  pl.BlockSpec(rest.shape, const),
                  pl.BlockSpec(expand.shape, const),
                  pl.BlockSpec(eye.shape, const)],
        out_specs=pl.BlockSpec((seq, GROUP_W), lambda b, g: (b, g)),
        out_shape=jax.ShapeDtypeStruct((batch * seq, D_INNER), BF16),
        scratch_shapes=[pltpu.VMEM((seq, GROUP_W), BF16),
                        pltpu.VMEM((seq, D_STATE), BF16),
                        pltpu.VMEM((seq, D_STATE), BF16),
                        pltpu.VMEM((seq, GROUP_W), BF16),
                        pltpu.VMEM((D_STATE, GROUP_W), F32),
                        pltpu.VMEM((D_STATE, GROUP_W), F32)],
        compiler_params=pltpu.CompilerParams(
            dimension_semantics=("parallel", "parallel"), vmem_limit_bytes=VMEM_LIMIT_BYTES),
        name="ssd",
    )(proj, proj, proj, proj, lhsp, rowpk, conv_w, conv_b, conv_w, conv_b, conv_w, conv_b,
      dskip_e, norm_w, h0, rest, expand, eye)


CONF_LANE_BLOCK = 128


N_TAIL_PROJ = 5


def _tail_kernel(yn_ref, x_ref, xnext_ref, mod_ref, modnext_ref, nw_ref, wt_ref, cw_ref, cb_ref,
                 lnw_ref, lnb_ref, wos_ref, woc_ref, wo_ref, fnw_ref, o_ref,
                 upad_s, uc_s, proj_s, act_s, hnext_s):
    tm = x_ref.shape[0]
    nseq = tm // GRID_W

    def project(t, h):
        proj_s[t] = jnp.dot(h, wt_ref[t], preferred_element_type=F32)

    @pl.when(pl.program_id(0) == 0)
    def _():
        upad_s[...] = jnp.zeros_like(upad_s)
        h0 = _modulated_norm(x_ref[...], nw_ref[...], mod_ref).astype(BF16)
        for t in range(N_TAIL_PROJ):
            project(t, h0)

    u = proj_s[0] * _sigmoid(proj_s[1])
    for q in range(nseq):
        upad_s[pl.ds(q * CONF_SLOT + CONF_PAD_ROWS, GRID_W), :] = u[q * GRID_W:(q + 1) * GRID_W]
    act_s[0] = _silu(proj_s[2])
    act_s[1] = _sigmoid(proj_s[3])
    act_s[2] = _sigmoid(proj_s[4])
    hnext_s[...] = _modulated_norm(xnext_ref[...], nw_ref[...], modnext_ref).astype(BF16)

    first_tap = CONF_PAD_ROWS - CONF_KERNEL // 2

    hn = hnext_s[...]
    for t in range(N_TAIL_PROJ):
        project(t, hn)
    for q in range(nseq):
        for j in range(D_CONF // CONF_LANE_BLOCK):
            lanes = pl.ds(j * CONF_LANE_BLOCK, CONF_LANE_BLOCK)
            acc = jnp.broadcast_to(cb_ref[:, lanes], (GRID_W, CONF_LANE_BLOCK))
            padded = upad_s[pl.ds(q * CONF_SLOT, CONF_SLOT), lanes]
            for r in range(SUBLANES):
                shifted = padded if r == 0 else pltpu.roll(padded, CONF_SLOT - r, 0)
                for a in range((CONF_SLOT - GRID_W) // SUBLANES):
                    k = SUBLANES * a + r - first_tap
                    if 0 <= k < CONF_KERNEL:
                        acc = acc + shifted[SUBLANES * a:SUBLANES * a + GRID_W] * cw_ref[k:k + 1, lanes]
            uc_s[pl.ds(q * GRID_W, GRID_W), lanes] = acc
    x = x_ref[...]
    uc = uc_s[...]
    mu = jnp.mean(uc, axis=-1, keepdims=True)
    dev = uc - mu
    var = jnp.mean(dev * dev, axis=-1, keepdims=True)
    ln = dev * lax.rsqrt(var + EPS) * lnw_ref[...] + lnb_ref[...]
    u2 = _silu(ln) * act_s[0]
    branch_conf = jnp.dot(u2.astype(BF16), woc_ref[...], preferred_element_type=F32)
    branch_ssm = jnp.dot(yn_ref[...], wos_ref[...], preferred_element_type=F32)
    merged = act_s[1] * branch_ssm + act_s[2] * branch_conf
    out = jnp.dot(merged.astype(BF16), wo_ref[...], preferred_element_type=F32)
    xn = x + mod_ref[0, 2:3, :] * out
    ms = jnp.mean(xn * xn, axis=-1, keepdims=True)
    o_ref[...] = xn * lax.rsqrt(ms + EPS) * fnw_ref[...]


def _tail(yn, x2d, mod3, norm_w, w_tail, conf_w, conf_b, ln_w, ln_b, w_os, w_oc, w_o, fn_w, *,
          tm, seq):
    m, d = x2d.shape
    tiles_per_batch = seq // tm
    const = lambda i: (0, 0)
    n_tiles = m // tm
    nxt = lambda i: jnp.minimum(i + 1, n_tiles - 1)
    once = dict(pipeline_mode=pl.Buffered(1))
    return pl.pallas_call(
        _tail_kernel,
        grid=(m // tm,),
        in_specs=[pl.BlockSpec((tm, D_INNER), lambda i: (i, 0)),
                  pl.BlockSpec((tm, d), lambda i: (i, 0)),
                  pl.BlockSpec((tm, d), lambda i: (nxt(i), 0)),
                  pl.BlockSpec((1, 3, d), lambda i: (i // tiles_per_batch, 0, 0)),
                  pl.BlockSpec((1, 3, d), lambda i: (nxt(i) // tiles_per_batch, 0, 0)),
                  pl.BlockSpec((1, d), const),
                  pl.BlockSpec(w_tail.shape, lambda i: (0, 0, 0), **once),
                  pl.BlockSpec((CONF_KERNEL, D_CONF), const),
                  pl.BlockSpec((1, D_CONF), const),
                  pl.BlockSpec((1, D_CONF), const),
                  pl.BlockSpec((1, D_CONF), const),
                  pl.BlockSpec((D_INNER, D_MODEL), const, **once),
                  pl.BlockSpec((D_CONF, D_MODEL), const, **once),
                  pl.BlockSpec((D_MODEL, D_MODEL), const, **once),
                  pl.BlockSpec((1, D_MODEL), const)],
        out_specs=pl.BlockSpec((tm, d), lambda i: (i, 0)),
        out_shape=jax.ShapeDtypeStruct((m, d), F32),
        scratch_shapes=[pltpu.VMEM((tm // GRID_W * CONF_SLOT, D_CONF), F32),
                        pltpu.VMEM((tm, D_CONF), F32),
                        pltpu.VMEM((N_TAIL_PROJ, tm, D_CONF), F32),
                        pltpu.VMEM((3, tm, D_CONF), F32),
                        pltpu.VMEM((tm, D_MODEL), BF16)],
        compiler_params=pltpu.CompilerParams(
            dimension_semantics=("arbitrary",), vmem_limit_bytes=VMEM_LIMIT_BYTES),
        name="tail",
    )(yn, x2d, x2d, mod3, mod3, norm_w, w_tail, conf_w, conf_b, ln_w, ln_b, w_os, w_oc, w_o, fn_w)


def _head_perm():
    return np.array([d * N_HEADS + g * HPG + r
                     for g in range(N_GROUPS) for d in range(2) for r in range(HPG)])


def kernel(x, c, ctx, c_ctx, w_mod, b_mod, norm_w, w_in, ssm_conv_w, ssm_conv_b, dt_bias, a_log,
           d_skip, ssm_norm_w, w_out_ssm, conf_conv_w, conf_conv_b, conf_ln_w, conf_ln_b,
           w_out_conf, w_out, final_norm_w):
    batch, seq, d = x.shape
    ctx_len = ctx.shape[1]
    assert w_in.shape[0] == 1, "single trunk layer"
    assert d == D_MODEL and seq % CHUNK == 0 and ctx_len % CHUNK == 0 and batch + 1 <= 8

    w_in0 = w_in[0]
    w_main = jnp.concatenate([w_in0[:, :C_END], w_in0[:, DT_END:Z_END]], axis=1).astype(BF16)
    w_tail = w_in0[:, Z_END:].astype(BF16).reshape(d, N_TAIL_PROJ, D_CONF).transpose(1, 0, 2)
    perm = _head_perm()
    pad_heads = DT_LANES - 2 * N_HEADS
    w_dt = jnp.pad(w_in0[:, C_END:DT_END][:, perm], ((0, 0), (0, pad_heads))).astype(BF16)
    bias_p = jnp.pad(dt_bias[0].reshape(-1)[perm], (0, pad_heads)).reshape(1, DT_LANES)
    alog_p = jnp.pad(a_log[0].reshape(-1)[perm], (0, pad_heads)).reshape(1, DT_LANES)
    conv_w = ssm_conv_w[0]
    conv_b = ssm_conv_b[0].reshape(1, C_END)
    dskip_e = jnp.repeat(d_skip[0], HEAD_DIM).reshape(1, D_INNER)
    nw2 = norm_w[0].reshape(1, d)
    rest, expand, eye, scatter = _decay_constants()

    c_all = jnp.concatenate([c, c_ctx[None, :], jnp.zeros((8 - batch - 1, d), F32)], axis=0)
    mod3 = _mod(c_all, w_mod[0], b_mod[0]).reshape(8, 3, d)

    ctx_rows = batch * ctx_len
    tm_ctx = min(IN_PROJ_TM, ctx_rows)
    proj_ctx, dtraw_ctx = _in_proj(ctx.reshape(ctx_rows, d), mod3, nw2, w_main, w_dt,
                                   tm=tm_ctx, n_cols=B_END, plain_cols=B_END,
                                   mod_row=lambda i: batch)
    lhsp_ctx, rowpk_ctx = _dt_prep(dtraw_ctx, bias_p, alog_p, scatter, batch, ctx_len)
    h0 = _ctx_states(proj_ctx, lhsp_ctx, rowpk_ctx, conv_w, conv_b, expand, batch, ctx_len)

    m = batch * seq
    x2d = x.reshape(m, d)
    tm = min(IN_PROJ_TM, seq)
    tiles_per_batch = seq // tm
    proj, dtraw = _in_proj(x2d, mod3, nw2, w_main, w_dt, tm=tm, n_cols=P_COLS, plain_cols=C_END,
                           mod_row=lambda i: i // tiles_per_batch)
    lhsp, rowpk = _dt_prep(dtraw, bias_p, alog_p, scatter, batch, seq)
    yn = _ssd(proj, lhsp, rowpk, conv_w, conv_b, dskip_e, ssm_norm_w[0].reshape(1, D_INNER), h0,
              rest, expand, eye, batch, seq)
    out = _tail(yn, x2d, mod3, nw2, w_tail, conf_conv_w[0], conf_conv_b[0].reshape(1, D_CONF),
                conf_ln_w[0].reshape(1, D_CONF), conf_ln_b[0].reshape(1, D_CONF),
                w_out_ssm[0].astype(BF16), w_out_conf[0].astype(BF16), w_out[0].astype(BF16),
                final_norm_w.reshape(1, d), tm=min(256, seq), seq=seq)
    return out.reshape(batch, seq, d)
```

```python
import functools

import numpy as np
import jax
import jax.numpy as jnp
from jax import lax
from jax.experimental import pallas as pl
from jax.experimental.pallas import tpu as pltpu

F32 = jnp.float32
BF16 = jnp.bfloat16

D_MODEL = 1024
GRID_W = 64
D_INNER = 2 * D_MODEL
HEAD_DIM = 64
N_HEADS = D_INNER // HEAD_DIM
N_GROUPS = 8
HPG = N_HEADS // N_GROUPS
D_STATE = 128
SSM_CONV = 4
CHUNK = 128
D_CONF = D_MODEL
CONF_KERNEL = 31
EPS = 1e-6

GN = N_GROUPS * D_STATE
X_END = D_INNER
B_END = X_END + GN
C_END = B_END + GN
DT_END = C_END + 2 * N_HEADS
Z_END = DT_END + D_INNER
GLU_END = Z_END + 2 * D_CONF
CG_END = GLU_END + D_CONF
IN_COLS = CG_END + 2 * D_MODEL

GROUP_W = HPG * HEAD_DIM
DT_LANES = 128
P_Z = C_END
P_COLS = C_END + D_INNER
IN_PROJ_TN = 1024
IN_PROJ_TM = 512

HALO = 16
VMEM_LIMIT_BYTES = 56 * 1024 * 1024


def _sigmoid(v):
    return 1.0 / (1.0 + jnp.exp(-v))


def _silu(v):
    return v * _sigmoid(v)


def _mod_kernel(c_ref, w_ref, b_ref, o_ref):
    s = _silu(c_ref[...])
    o_ref[...] = jnp.dot(s, w_ref[...], preferred_element_type=F32,
                         precision=lax.Precision.HIGHEST) + b_ref[...]


def _mod(c_all, w_mod, b_mod):
    rows, d = c_all.shape
    n = w_mod.shape[1]
    tn = 1024
    return pl.pallas_call(
        _mod_kernel,
        grid=(n // tn,),
        in_specs=[pl.BlockSpec((rows, d), lambda j: (0, 0)),
                  pl.BlockSpec((d, tn), lambda j: (0, j)),
                  pl.BlockSpec((1, tn), lambda j: (0, j))],
        out_specs=pl.BlockSpec((rows, tn), lambda j: (0, j)),
        out_shape=jax.ShapeDtypeStruct((rows, n), F32),
        name="mod",
    )(c_all, w_mod, b_mod.reshape(1, n))


def _modulated_norm(x, nw, mod_ref):
    ms = jnp.mean(x * x, axis=-1, keepdims=True)
    y = x * lax.rsqrt(ms + EPS) * nw
    return y * (1.0 + mod_ref[0, 1:2, :]) + mod_ref[0, 0:1, :]


def _inproj_kernel(x_ref, mod_ref, nw_ref, w_ref, wdt_ref, o_ref, dt_ref, *, plain_cols, tn):
    hb = _modulated_norm(x_ref[...], nw_ref[...], mod_ref).astype(BF16)
    dt_ref[...] = jnp.dot(hb, wdt_ref[...], preferred_element_type=F32)
    for c0 in range(0, w_ref.shape[1], tn):
        acc = jnp.dot(hb, w_ref[:, c0:c0 + tn], preferred_element_type=F32)
        o_ref[:, c0:c0 + tn] = (acc if c0 < plain_cols else _silu(acc)).astype(BF16)


def _in_proj(x2d, mod3, norm_w, w_main, w_dt, *, tm, n_cols, plain_cols, mod_row):
    m, d = x2d.shape
    const = lambda i: (0, 0)
    once = dict(pipeline_mode=pl.Buffered(1))
    return pl.pallas_call(
        functools.partial(_inproj_kernel, plain_cols=plain_cols, tn=IN_PROJ_TN),
        grid=(m // tm,),
        in_specs=[pl.BlockSpec((tm, d), lambda i: (i, 0)),
                  pl.BlockSpec((1, 3, d), lambda i: (mod_row(i), 0, 0)),
                  pl.BlockSpec((1, d), const),
                  pl.BlockSpec((d, n_cols), const, **once),
                  pl.BlockSpec((d, DT_LANES), const, **once)],
        out_specs=[pl.BlockSpec((tm, n_cols), lambda i: (i, 0)),
                   pl.BlockSpec((tm, DT_LANES), lambda i: (i, 0))],
        out_shape=[jax.ShapeDtypeStruct((m, n_cols), BF16),
                   jax.ShapeDtypeStruct((m, DT_LANES), F32)],
        compiler_params=pltpu.CompilerParams(
            dimension_semantics=("parallel",), vmem_limit_bytes=VMEM_LIMIT_BYTES),
        name="in_proj",
    )(x2d, mod3, norm_w, w_main, w_dt)


N_PIECES = 3
HD = 2 * HPG
LHS_ONES = 0
LHS_LA = 32
ROW_WEND = N_PIECES * HD
ROWS_PER_GROUP = ROW_WEND + HD
MASK_BIG = 1e30
DT_FLOOR = 1e-37
LOG2E = 1.4426950408889634


def _decay_constants():
    k = np.arange(CHUNK)[:, None]
    s = np.arange(CHUNK)[None, :]
    rest = np.zeros((2 * CHUNK, HD * CHUNK), np.float32)
    expand = np.zeros((CHUNK, 2 * GROUP_W), np.float32)
    for h in range(HD):
        cols = slice(h * CHUNK, (h + 1) * CHUNK)
        hidden = (s > k) if h < HPG else (s < k)
        rest[CHUNK:, cols] = np.where(hidden, -MASK_BIG, 0.0)
        for j in range(N_PIECES):
            rest[LHS_LA + HD * j + h, cols] = 1.0
            expand[LHS_LA + HD * j + h, h * HEAD_DIM:(h + 1) * HEAD_DIM] = 1.0
    scatter = np.zeros((4 * DT_LANES, N_GROUPS * CHUNK), np.float32)
    for g in range(N_GROUPS):
        scatter[N_PIECES * DT_LANES, g * CHUNK + LHS_ONES:g * CHUNK + LHS_ONES + ROW_WEND] = 1.0
        for h in range(HD):
            for j in range(N_PIECES):
                scatter[j * DT_LANES + HD * g + h, g * CHUNK + LHS_LA + HD * j + h] = 1.0
    return (jnp.asarray(rest[ROWS_PER_GROUP:], BF16), jnp.asarray(expand, BF16),
            jnp.asarray(np.eye(CHUNK), BF16), jnp.asarray(scatter, BF16))


def _split3(v):
    p0 = v.astype(BF16)
    r1 = v - p0.astype(F32)
    p1 = r1.astype(BF16)
    p2 = (r1 - p1.astype(F32)).astype(BF16)
    return p0, p1, p2


def _dt_kernel(raw_ref, bias_ref, alog_ref, scatter_ref, lhsp_ref, rowpk_ref):
    n_chunks = raw_ref.shape[0] // CHUNK
    bias = bias_ref[...]
    a = -jnp.exp(alog_ref[...])
    row = lax.broadcasted_iota(jnp.int32, (CHUNK, CHUNK), 0)
    col = lax.broadcasted_iota(jnp.int32, (CHUNK, CHUNK), 1)
    tri_lo = (col <= row).astype(BF16)
    tri_up = (col >= row).astype(BF16)
    lane = lax.broadcasted_iota(jnp.int32, (1, DT_LANES), 1)
    is_fwd = (lane % HD) < HPG
    ones = jnp.ones((CHUNK, DT_LANES), BF16)

    def cumulate(tri, pieces):
        acc = jnp.dot(tri, pieces, preferred_element_type=F32)
        return acc[:, :DT_LANES] + acc[:, DT_LANES:2 * DT_LANES] + acc[:, 2 * DT_LANES:]

    for ci in range(n_chunks):
        rows = pl.ds(ci * CHUNK, CHUNK)
        v = raw_ref[rows, :] + bias
        dt = jnp.maximum(v, 0.0) + jnp.log1p(jnp.exp(-jnp.abs(v)))
        dta3 = jnp.concatenate(_split3(dt * a), axis=1)
        la = jnp.where(is_fwd, cumulate(tri_lo, dta3), cumulate(tri_up, dta3))
        la_end = jnp.where(is_fwd, la[CHUNK - 1:CHUNK, :], la[0:1, :])
        wend = jnp.exp(la_end - la) * dt
        la2 = la * LOG2E
        nla = jnp.log2(jnp.maximum(dt, DT_FLOOR)) - la2
        lhs = jnp.concatenate(list(_split3(la2)) + [ones], axis=1)
        lhsp_ref[rows, :] = jnp.dot(lhs, scatter_ref[...],
                                    preferred_element_type=F32).astype(lhsp_ref.dtype)
        pieces = [p.astype(F32) for p in _split3(nla.T)] + [wend.T]
        for g in range(N_GROUPS):
            for j, val in enumerate(pieces):
                rowpk_ref[0, pl.ds(ROWS_PER_GROUP * g + HD * j, HD), rows] = val[HD * g:HD * (g + 1), :]


DT_CHUNKS_PER_STEP = 8


def _dt_prep(dt_raw, bias_p, alog_p, scatter, batch, seq):
    nc = seq // CHUNK
    per = min(DT_CHUNKS_PER_STEP, nc)
    steps = nc // per
    rows = per * CHUNK
    const = lambda b, c: (0, 0)
    return pl.pallas_call(
        _dt_kernel,
        grid=(batch, steps),
        in_specs=[pl.BlockSpec((rows, DT_LANES), lambda b, c: (b * steps + c, 0)),
                  pl.BlockSpec((1, DT_LANES), const),
                  pl.BlockSpec((1, DT_LANES), const),
                  pl.BlockSpec(scatter.shape, const)],
        out_specs=[pl.BlockSpec((rows, N_GROUPS * CHUNK), lambda b, c: (b * steps + c, 0)),
                   pl.BlockSpec((1, N_GROUPS * ROWS_PER_GROUP, rows), lambda b, c: (b, 0, c))],
        out_shape=[jax.ShapeDtypeStruct((batch * seq, N_GROUPS * CHUNK), BF16),
                   jax.ShapeDtypeStruct((batch, N_GROUPS * ROWS_PER_GROUP, seq), F32)],
        compiler_params=pltpu.CompilerParams(dimension_semantics=("parallel", "parallel")),
        name="dt_prep",
    )(dt_raw, bias_p, alog_p, scatter)


def _chunk_start(c):
    return c * CHUNK if isinstance(c, int) else pl.multiple_of(c * CHUNK, CHUNK)


def _conv_silu_chunk(refs, c, nc, w, b):
    seq = refs[0].shape[0]
    r0 = _chunk_start(c)
    if isinstance(c, int):
        p0, n0 = max(r0 - HALO, 0), min(r0 + CHUNK, seq - HALO)
    else:
        p0 = pl.multiple_of(jnp.maximum(r0 - HALO, 0), HALO)
        n0 = pl.multiple_of(jnp.minimum(r0 + CHUNK, seq - HALO), HALO)

    def rows(start, size):
        return jnp.concatenate([ref[pl.ds(start, size), :] for ref in refs], axis=1).astype(F32)

    prev, main, nxt = rows(p0, HALO), rows(r0, CHUNK), rows(n0, HALO)
    prev = jnp.where(c > 0, prev, 0.0)
    nxt = jnp.where(c < nc - 1, nxt, 0.0)
    ext = jnp.concatenate([prev, main, nxt], axis=0)
    total = CHUNK + 2 * HALO
    xm2 = pltpu.roll(ext, 2, 0)[HALO:HALO + CHUNK]
    xm1 = pltpu.roll(ext, 1, 0)[HALO:HALO + CHUNK]
    xp1 = pltpu.roll(ext, total - 1, 0)[HALO:HALO + CHUNK]
    y = w[0:1] * xm2 + w[1:2] * xm1 + w[2:3] * main + w[3:4] * xp1 + b
    return _silu(y)


def _chunk_operands(lhsp_ref, rowpk_ref, c):
    r0 = _chunk_start(c)
    return lhsp_ref[pl.ds(r0, CHUNK), :], rowpk_ref[0, :, pl.ds(r0, CHUNK)]


def _head_decay(lt, expand_ref, direction=None):
    if direction is None:
        ex = expand_ref[...]
    else:
        ex = expand_ref[:, direction * GROUP_W:(direction + 1) * GROUP_W]
    return jnp.exp2(jnp.dot(lt, ex, preferred_element_type=F32))


def _xs_blockdiag(xs_bf):
    lane = lax.broadcasted_iota(jnp.int32, xs_bf.shape, 1)
    zero = jnp.zeros_like(xs_bf)
    return jnp.concatenate(
        [jnp.where((lane >= r * HEAD_DIM) & (lane < (r + 1) * HEAD_DIM), xs_bf, zero)
         for r in range(HPG)], axis=0)


def _bt_weighted(bt_bf, rp, direction):
    base = ROW_WEND + direction * HPG
    w = rp[base:base + HPG, :].astype(BF16)
    return jnp.concatenate([bt_bf * w[r:r + 1, :] for r in range(HPG)], axis=1)


def _ctx_kernel(xs_ref, b_ref, lhsp_ref, rowpk_ref, cwx_ref, cbx_ref, cwb_ref, cbb_ref,
                expand_ref, o_ref):
    nc = xs_ref.shape[0] // CHUNK
    cw = jnp.concatenate([cwx_ref[...], cwb_ref[...]], axis=1)
    cb = jnp.concatenate([cbx_ref[...], cbb_ref[...]], axis=1)
    contrib, decay = [], []
    for c in range(nc):
        xb = _conv_silu_chunk((xs_ref, b_ref), c, nc, cw, cb)
        xs_bf = xb[:, :GROUP_W].astype(BF16)
        bt = xb[:, GROUP_W:].T.astype(BF16)
        lt, rp = _chunk_operands(lhsp_ref, rowpk_ref, c)
        lhs = jnp.concatenate([_bt_weighted(bt, rp, 0), _bt_weighted(bt, rp, 1)], axis=0)
        contrib.append(jnp.dot(lhs, _xs_blockdiag(xs_bf), preferred_element_type=F32))
        decay.append(_head_decay(lt, expand_ref))
    hf = jnp.zeros((D_STATE, GROUP_W), F32)
    for c in range(nc):
        hf = hf * decay[c][CHUNK - 1:CHUNK, :GROUP_W] + contrib[c][:D_STATE]
    hb = jnp.zeros((D_STATE, GROUP_W), F32)
    for c in reversed(range(nc)):
        hb = hb * decay[c][0:1, GROUP_W:] + contrib[c][D_STATE:]
    o_ref[0, 0, 0] = hf
    o_ref[0, 0, 1] = hb


def _ctx_states(proj_ctx, lhsp, rowpk, conv_w, conv_b, expand, batch, seq):
    bcol = X_END // D_STATE
    const = lambda b, g: (0, 0)
    return pl.pallas_call(
        _ctx_kernel,
        grid=(batch, N_GROUPS),
        in_specs=[pl.BlockSpec((seq, GROUP_W), lambda b, g: (b, g)),
                  pl.BlockSpec((seq, D_STATE), lambda b, g: (b, bcol + g)),
                  pl.BlockSpec((seq, CHUNK), lambda b, g: (b, g)),
                  pl.BlockSpec((1, ROWS_PER_GROUP, seq), lambda b, g: (b, g, 0)),
                  pl.BlockSpec((SSM_CONV, GROUP_W), lambda b, g: (0, g)),
                  pl.BlockSpec((1, GROUP_W), lambda b, g: (0, g)),
                  pl.BlockSpec((SSM_CONV, D_STATE), lambda b, g: (0, bcol + g)),
                  pl.BlockSpec((1, D_STATE), lambda b, g: (0, bcol + g)),
                  pl.BlockSpec(expand.shape, const)],
        out_specs=pl.BlockSpec((1, 1, 2, D_STATE, GROUP_W), lambda b, g: (b, g, 0, 0, 0)),
        out_shape=jax.ShapeDtypeStruct((batch, N_GROUPS, 2, D_STATE, GROUP_W), F32),
        compiler_params=pltpu.CompilerParams(dimension_semantics=("parallel", "parallel")),
        name="ctx_states",
    )(proj_ctx, proj_ctx, lhsp, rowpk, conv_w, conv_b, conv_w, conv_b, expand)


def _ssd_kernel(xs_ref, b_ref, c_ref, z_ref, lhsp_ref, rowpk_ref, cwx_ref, cbx_ref, cwb_ref,
                cbb_ref, cwc_ref, cbc_ref, dsk_ref, nw_ref, h0_ref, rest_ref, expand_ref, eye_ref,
                o_ref, cx_s, cbt_s, cc_s, stash_s, hf_s, hb_s):
    seq = xs_ref.shape[0]
    nc = seq // CHUNK
    cw = jnp.concatenate([cwx_ref[...], cwb_ref[...], cwc_ref[...]], axis=1)
    cb = jnp.concatenate([cbx_ref[...], cbb_ref[...], cbc_ref[...]], axis=1)
    hf_s[...] = h0_ref[0, 0, 0]
    hb_s[...] = h0_ref[0, 0, 1]

    def back_body(i, carry):
        c = nc - 1 - i
        r0 = _chunk_start(c)
        xbc = _conv_silu_chunk((xs_ref, b_ref, c_ref), c, nc, cw, cb)
        xs_bf = xbc[:, :GROUP_W].astype(BF16)
        bt = xbc[:, GROUP_W:GROUP_W + D_STATE].T.astype(BF16)
        cx_s[pl.ds(r0, CHUNK), :] = xs_bf
        cbt_s[pl.ds(r0, CHUNK), :] = bt
        cc_s[pl.ds(r0, CHUNK), :] = xbc[:, GROUP_W + D_STATE:].astype(BF16)
        lt, rp = _chunk_operands(lhsp_ref, rowpk_ref, c)
        hb = hb_s[...]
        stash_s[pl.ds(r0, CHUNK), :] = hb.astype(BF16)
        dec = _head_decay(lt, expand_ref, 1)[0:1, :]
        contrib = jnp.dot(_bt_weighted(bt, rp, 1), _xs_blockdiag(xs_bf), preferred_element_type=F32)
        hb_s[...] = hb * dec + contrib
        return carry

    lax.fori_loop(0, nc, back_body, 0, unroll=4)

    prow = lax.broadcasted_iota(jnp.int32, (ROWS_PER_GROUP, HD * CHUNK), 0)
    pcol = lax.broadcasted_iota(jnp.int32, (ROWS_PER_GROUP, HD * CHUNK), 1)
    own_block = (prow < ROW_WEND) & (prow % HD == pcol // CHUNK)
    dsk, nw = dsk_ref[...], nw_ref[...]

    def fwd_body(c, carry):
        r0 = _chunk_start(c)
        xs_bf = cx_s[pl.ds(r0, CHUNK), :]
        bt = cbt_s[pl.ds(r0, CHUNK), :]
        cc = cc_s[pl.ds(r0, CHUNK), :]
        lt, rp = _chunk_operands(lhsp_ref, rowpk_ref, c)
        scores = jnp.dot(cc, bt, preferred_element_type=F32)
        rows = jnp.concatenate([rp.astype(BF16)] * HD, axis=1)
        rows = jnp.where(own_block, rows, jnp.zeros_like(rows))
        seg = jnp.dot(jnp.concatenate([lt, eye_ref[...]], axis=1),
                      jnp.concatenate([rows, rest_ref[...]], axis=0), preferred_element_type=F32)
        decay_dt = jnp.exp2(seg)
        mix = jnp.concatenate(
            [(scores * (decay_dt[:, r * CHUNK:(r + 1) * CHUNK]
                        + decay_dt[:, (HPG + r) * CHUNK:(HPG + r + 1) * CHUNK])).astype(BF16)
             for r in range(HPG)], axis=1)
        lhs = jnp.concatenate([mix, _bt_weighted(bt, rp, 0)], axis=0)
        prod = jnp.dot(lhs, _xs_blockdiag(xs_bf), preferred_element_type=F32)
        hf = hf_s[...]
        e = _head_decay(lt, expand_ref)
        states = jnp.concatenate([hf.astype(BF16), stash_s[pl.ds(r0, CHUNK), :]], axis=1)
        carried = jnp.dot(cc, states, preferred_element_type=F32) * e
        y = dsk * xs_bf.astype(F32) + prod[:CHUNK] + carried[:, :GROUP_W] + carried[:, GROUP_W:]
        hf_s[...] = hf * e[CHUNK - 1:CHUNK, :GROUP_W] + prod[CHUNK:]
        y = y * z_ref[pl.ds(r0, CHUNK), :].astype(F32)
        ms = jnp.mean(y * y, axis=-1, keepdims=True)
        o_ref[pl.ds(r0, CHUNK), :] = (y * lax.rsqrt(ms + EPS) * nw).astype(o_ref.dtype)
        return carry

    lax.fori_loop(0, nc, fwd_body, 0, unroll=4)


def _ssd(proj, lhsp, rowpk, conv_w, conv_b, dskip_e, norm_w, h0, rest, expand, eye, batch, seq):
    bcol = X_END // D_STATE
    ccol = B_END // D_STATE
    zcol = P_Z // GROUP_W
    gmap = lambda b, g: (0, g)
    const = lambda b, g: (0, 0)
    return pl.pallas_call(
        _ssd_kernel,
        grid=(batch, N_GROUPS),
        in_specs=[pl.BlockSpec((seq, GROUP_W), lambda b, g: (b, g)),
                  pl.BlockSpec((seq, D_STATE), lambda b, g: (b, bcol + g)),
                  pl.BlockSpec((seq, D_STATE), lambda b, g: (b, ccol + g)),
                  pl.BlockSpec((seq, GROUP_W), lambda b, g: (b, zcol + g)),
                  pl.BlockSpec((seq, CHUNK), lambda b, g: (b, g)),
                  pl.BlockSpec((1, ROWS_PER_GROUP, seq), lambda b, g: (b, g, 0)),
                  pl.BlockSpec((SSM_CONV, GROUP_W), gmap),
                  pl.BlockSpec((1, GROUP_W), gmap),
                  pl.BlockSpec((SSM_CONV, D_STATE), lambda b, g: (0, bcol + g)),
                  pl.BlockSpec((1, D_STATE), lambda b, g: (0, bcol + g)),
                  pl.BlockSpec((SSM_CONV, D_STATE), lambda b, g: (0, ccol + g)),
                  pl.BlockSpec((1, D_STATE), lambda b, g: (0, ccol + g)),
                  pl.BlockSpec((1, GROUP_W), gmap),
                  pl.BlockSpec((1, GROUP_W), gmap),
                  pl.BlockSpec((1, 1, 2, D_STATE, GROUP_W), lambda b, g: (b, g, 0, 0, 0)),
                  pl.BlockSpec(rest.shape, const),
                  pl.BlockSpec(expand.shape, const),
                  pl.BlockSpec(eye.shape, const)],
        out_specs=pl.BlockSpec((seq, GROUP_W), lambda b, g: (b, g)),
        out_shape=jax.ShapeDtypeStruct((batch * seq, D_INNER), BF16),
        scratch_shapes=[pltpu.VMEM((seq, GROUP_W), BF16),
                        pltpu.VMEM((seq, D_STATE), BF16),
                        pltpu.VMEM((seq, D_STATE), BF16),
                        pltpu.VMEM((seq, GROUP_W), BF16),
                        pltpu.VMEM((D_STATE, GROUP_W), F32),
                        pltpu.VMEM((D_STATE, GROUP_W), F32)],
        compiler_params=pltpu.CompilerParams(
            dimension_semantics=("parallel", "parallel"), vmem_limit_bytes=VMEM_LIMIT_BYTES),
        name="ssd",
    )(proj, proj, proj, proj, lhsp, rowpk, conv_w, conv_b, conv_w, conv_b, conv_w, conv_b,
      dskip_e, norm_w, h0, rest, expand, eye)


N_TAIL_PROJ = 5


DFT_N = 2 * GRID_W
DFT_HALF = DFT_N // 2
CONF_TAPS_PADDED = 32


def _conv_dft_constants():
    t = np.arange(GRID_W)
    f = np.arange(DFT_HALF)
    ang = 2.0 * np.pi * np.outer(f, t) / DFT_N
    fwd = np.zeros((DFT_N, GRID_W))
    fwd[:DFT_HALF] = np.cos(ang)
    fwd[DFT_HALF] = np.cos(np.pi * t)
    fwd[DFT_HALF + 1:] = -np.sin(ang[1:])
    inv = np.zeros((GRID_W, DFT_N))
    inv[:, 0] = 1.0 / DFT_N
    inv[:, 1:DFT_HALF] = (2.0 / DFT_N) * np.cos(ang[1:].T)
    inv[:, DFT_HALF] = np.cos(np.pi * t) / DFT_N
    inv[:, DFT_HALF + 1:] = -(2.0 / DFT_N) * np.sin(ang[1:].T)
    lag = CONF_KERNEL // 2 - np.arange(CONF_KERNEL)
    angh = 2.0 * np.pi * np.outer(np.arange(DFT_HALF + 1), lag) / DFT_N
    hre, him = np.cos(angh), -np.sin(angh)
    taps = np.zeros((3 * DFT_HALF, CONF_TAPS_PADDED))
    taps[:DFT_HALF, :CONF_KERNEL] = hre[:DFT_HALF]
    taps[DFT_HALF:2 * DFT_HALF, :CONF_KERNEL] = hre[:DFT_HALF]
    taps[DFT_HALF, :CONF_KERNEL] = hre[DFT_HALF]
    taps[2 * DFT_HALF + 1:, :CONF_KERNEL] = him[1:DFT_HALF]
    return (jnp.asarray(fwd, F32), jnp.asarray(inv, F32), jnp.asarray(taps, F32))


def _tail_kernel(yn_ref, x_ref, xnext_ref, mod_ref, modnext_ref, nw_ref, wt_ref, cw_ref, cb_ref,
                 fwd_ref, inv_ref, taps_ref, lnw_ref, lnb_ref, wos_ref, woc_ref, wo_ref, fnw_ref,
                 o_ref, ub_s, uc_s, proj_s, act_s, coef_s):
    tm = x_ref.shape[0]
    nseq = tm // GRID_W

    def project(xv, mref):
        h = _modulated_norm(xv, nw_ref[...], mref).astype(BF16)
        for t in range(N_TAIL_PROJ):
            proj_s[t] = jnp.dot(h, wt_ref[t], preferred_element_type=F32)

    @pl.when(pl.program_id(0) == 0)
    def _():
        project(x_ref[...], mod_ref)
        coef_s[...] = jnp.dot(taps_ref[...], cw_ref[...], preferred_element_type=F32,
                              precision=lax.Precision.HIGHEST)

    ub_s[...] = (proj_s[0] * _sigmoid(proj_s[1])).astype(BF16)
    act_s[0] = _silu(proj_s[2])
    act_s[1] = _sigmoid(proj_s[3])
    act_s[2] = _sigmoid(proj_s[4])
    project(xnext_ref[...], modnext_ref)

    p_re, s_re, q_im = (coef_s[pl.ds(i * DFT_HALF, DFT_HALF), :] for i in range(3))
    for q in range(nseq):
        rows = pl.ds(q * GRID_W, GRID_W)
        spec = jnp.dot(fwd_ref[...], ub_s[rows, :], preferred_element_type=F32)
        top, bot = spec[:DFT_HALF], spec[DFT_HALF:]
        prod = jnp.concatenate([top * p_re - bot * q_im, top * q_im + bot * s_re], axis=0)
        uc_s[rows, :] = jnp.dot(inv_ref[...], prod.astype(BF16),
                                preferred_element_type=F32) + cb_ref[...]
    x = x_ref[...]
    uc = uc_s[...]
    mu = jnp.mean(uc, axis=-1, keepdims=True)
    dev = uc - mu
    var = jnp.mean(dev * dev, axis=-1, keepdims=True)
    ln = dev * lax.rsqrt(var + EPS) * lnw_ref[...] + lnb_ref[...]
    u2 = _silu(ln) * act_s[0]
    branch_conf = jnp.dot(u2.astype(BF16), woc_ref[...], preferred_element_type=F32)
    branch_ssm = jnp.dot(yn_ref[...], wos_ref[...], preferred_element_type=F32)
    merged = act_s[1] * branch_ssm + act_s[2] * branch_conf
    out = jnp.dot(merged.astype(BF16), wo_ref[...], preferred_element_type=F32)
    xn = x + mod_ref[0, 2:3, :] * out
    ms = jnp.mean(xn * xn, axis=-1, keepdims=True)
    o_ref[...] = xn * lax.rsqrt(ms + EPS) * fnw_ref[...]


def _tail(yn, x2d, mod3, norm_w, w_tail, conf_w, conf_b, ln_w, ln_b, w_os, w_oc, w_o, fn_w, *,
          tm, seq):
    m, d = x2d.shape
    tiles_per_batch = seq // tm
    const = lambda i: (0, 0)
    fwd, inv, taps = _conv_dft_constants()
    n_tiles = m // tm
    nxt = lambda i: jnp.minimum(i + 1, n_tiles - 1)
    once = dict(pipeline_mode=pl.Buffered(1))
    return pl.pallas_call(
        _tail_kernel,
        grid=(m // tm,),
        in_specs=[pl.BlockSpec((tm, D_INNER), lambda i: (i, 0)),
                  pl.BlockSpec((tm, d), lambda i: (i, 0)),
                  pl.BlockSpec((tm, d), lambda i: (nxt(i), 0)),
                  pl.BlockSpec((1, 3, d), lambda i: (i // tiles_per_batch, 0, 0)),
                  pl.BlockSpec((1, 3, d), lambda i: (nxt(i) // tiles_per_batch, 0, 0)),
                  pl.BlockSpec((1, d), const),
                  pl.BlockSpec(w_tail.shape, lambda i: (0, 0, 0), **once),
                  pl.BlockSpec((CONF_TAPS_PADDED, D_CONF), const),
                  pl.BlockSpec((1, D_CONF), const),
                  pl.BlockSpec(fwd.shape, const),
                  pl.BlockSpec(inv.shape, const),
                  pl.BlockSpec(taps.shape, const),
                  pl.BlockSpec((1, D_CONF), const),
                  pl.BlockSpec((1, D_CONF), const),
                  pl.BlockSpec((D_INNER, D_MODEL), const, **once),
                  pl.BlockSpec((D_CONF, D_MODEL), const, **once),
                  pl.BlockSpec((D_MODEL, D_MODEL), const, **once),
                  pl.BlockSpec((1, D_MODEL), const)],
        out_specs=pl.BlockSpec((tm, d), lambda i: (i, 0)),
        out_shape=jax.ShapeDtypeStruct((m, d), F32),
        scratch_shapes=[pltpu.VMEM((tm, D_CONF), BF16),
                        pltpu.VMEM((tm, D_CONF), F32),
                        pltpu.VMEM((N_TAIL_PROJ, tm, D_CONF), F32),
                        pltpu.VMEM((3, tm, D_CONF), F32),
                        pltpu.VMEM((3 * DFT_HALF, D_CONF), F32)],
        compiler_params=pltpu.CompilerParams(
            dimension_semantics=("arbitrary",), vmem_limit_bytes=VMEM_LIMIT_BYTES),
        name="tail",
    )(yn, x2d, x2d, mod3, mod3, norm_w, w_tail, conf_w, conf_b, fwd.astype(BF16),
      inv.astype(BF16), taps, ln_w, ln_b, w_os, w_oc, w_o, fn_w)


def _head_perm():
    return np.array([d * N_HEADS + g * HPG + r
                     for g in range(N_GROUPS) for d in range(2) for r in range(HPG)])


def kernel(x, c, ctx, c_ctx, w_mod, b_mod, norm_w, w_in, ssm_conv_w, ssm_conv_b, dt_bias, a_log,
           d_skip, ssm_norm_w, w_out_ssm, conf_conv_w, conf_conv_b, conf_ln_w, conf_ln_b,
           w_out_conf, w_out, final_norm_w):
    batch, seq, d = x.shape
    ctx_len = ctx.shape[1]
    assert w_in.shape[0] == 1, "single trunk layer"
    assert d == D_MODEL and seq % CHUNK == 0 and ctx_len % CHUNK == 0 and batch + 1 <= 8

    w_in0 = w_in[0]
    w_main = jnp.concatenate([w_in0[:, :C_END], w_in0[:, DT_END:Z_END]], axis=1).astype(BF16)
    w_tail = w_in0[:, Z_END:].astype(BF16).reshape(d, N_TAIL_PROJ, D_CONF).transpose(1, 0, 2)
    perm = _head_perm()
    pad_heads = DT_LANES - 2 * N_HEADS
    w_dt = jnp.pad(w_in0[:, C_END:DT_END][:, perm], ((0, 0), (0, pad_heads))).astype(BF16)
    bias_p = jnp.pad(dt_bias[0].reshape(-1)[perm], (0, pad_heads)).reshape(1, DT_LANES)
    alog_p = jnp.pad(a_log[0].reshape(-1)[perm], (0, pad_heads)).reshape(1, DT_LANES)
    conv_w = ssm_conv_w[0]
    conv_b = ssm_conv_b[0].reshape(1, C_END)
    dskip_e = jnp.repeat(d_skip[0], HEAD_DIM).reshape(1, D_INNER)
    nw2 = norm_w[0].reshape(1, d)
    rest, expand, eye, scatter = _decay_constants()

    c_all = jnp.concatenate([c, c_ctx[None, :], jnp.zeros((8 - batch - 1, d), F32)], axis=0)
    mod3 = _mod(c_all, w_mod[0], b_mod[0]).reshape(8, 3, d)

    ctx_rows = batch * ctx_len
    tm_ctx = min(IN_PROJ_TM, ctx_rows)
    proj_ctx, dtraw_ctx = _in_proj(ctx.reshape(ctx_rows, d), mod3, nw2, w_main, w_dt,
                                   tm=tm_ctx, n_cols=B_END, plain_cols=B_END,
                                   mod_row=lambda i: batch)
    lhsp_ctx, rowpk_ctx = _dt_prep(dtraw_ctx, bias_p, alog_p, scatter, batch, ctx_len)
    h0 = _ctx_states(proj_ctx, lhsp_ctx, rowpk_ctx, conv_w, conv_b, expand, batch, ctx_len)

    m = batch * seq
    x2d = x.reshape(m, d)
    tm = min(IN_PROJ_TM, seq)
    tiles_per_batch = seq // tm
    proj, dtraw = _in_proj(x2d, mod3, nw2, w_main, w_dt, tm=tm, n_cols=P_COLS, plain_cols=C_END,
                           mod_row=lambda i: i // tiles_per_batch)
    lhsp, rowpk = _dt_prep(dtraw, bias_p, alog_p, scatter, batch, seq)
    yn = _ssd(proj, lhsp, rowpk, conv_w, conv_b, dskip_e, ssm_norm_w[0].reshape(1, D_INNER), h0,
              rest, expand, eye, batch, seq)
    conf_w = jnp.pad(conf_conv_w[0], ((0, CONF_TAPS_PADDED - CONF_KERNEL), (0, 0)))
    out = _tail(yn, x2d, mod3, nw2, w_tail, conf_w, conf_conv_b[0].reshape(1, D_CONF),
                conf_ln_w[0].reshape(1, D_CONF), conf_ln_b[0].reshape(1, D_CONF),
                w_out_ssm[0].astype(BF16), w_out_conf[0].astype(BF16), w_out[0].astype(BF16),
                final_norm_w.reshape(1, d), tm=min(256, seq), seq=seq)
    return out.reshape(batch, seq, d)
```

```python
import functools

import numpy as np
import jax
import jax.numpy as jnp
from jax import lax
from jax.experimental import pallas as pl
from jax.experimental.pallas import tpu as pltpu

F32 = jnp.float32
BF16 = jnp.bfloat16

D_MODEL = 1024
GRID_W = 64
D_INNER = 2 * D_MODEL
HEAD_DIM = 64
N_HEADS = D_INNER // HEAD_DIM
N_GROUPS = 8
HPG = N_HEADS // N_GROUPS
D_STATE = 128
SSM_CONV = 4
CHUNK = 128
D_CONF = D_MODEL
CONF_KERNEL = 31
EPS = 1e-6

GN = N_GROUPS * D_STATE
X_END = D_INNER
B_END = X_END + GN
C_END = B_END + GN
DT_END = C_END + 2 * N_HEADS
Z_END = DT_END + D_INNER
GLU_END = Z_END + 2 * D_CONF
CG_END = GLU_END + D_CONF
IN_COLS = CG_END + 2 * D_MODEL

GROUP_W = HPG * HEAD_DIM
DT_LANES = 128
P_Z = C_END
P_COLS = C_END + D_INNER
IN_PROJ_TN = 1024
IN_PROJ_TM = 1024

HALO = 16
VMEM_LIMIT_BYTES = 58 * 1024 * 1024


def _sigmoid(v):
    return 1.0 / (1.0 + jnp.exp(-v))


def _silu(v):
    return v * _sigmoid(v)


def _mod_kernel(c_ref, w_ref, b_ref, o_ref):
    s = _silu(c_ref[...])
    o_ref[...] = jnp.dot(s, w_ref[...], preferred_element_type=F32,
                         precision=lax.Precision.HIGHEST) + b_ref[...]


def _mod(c_all, w_mod, b_mod):
    rows, d = c_all.shape
    n = w_mod.shape[1]
    tn = 1024
    return pl.pallas_call(
        _mod_kernel,
        grid=(n // tn,),
        in_specs=[pl.BlockSpec((rows, d), lambda j: (0, 0)),
                  pl.BlockSpec((d, tn), lambda j: (0, j)),
                  pl.BlockSpec((1, tn), lambda j: (0, j))],
        out_specs=pl.BlockSpec((rows, tn), lambda j: (0, j)),
        out_shape=jax.ShapeDtypeStruct((rows, n), F32),
        name="mod",
    )(c_all, w_mod, b_mod.reshape(1, n))


def _modulated_norm(x, nw, mod_ref):
    ms = jnp.mean(x * x, axis=-1, keepdims=True)
    y = x * lax.rsqrt(ms + EPS) * nw
    return y * (1.0 + mod_ref[0, 1:2, :]) + mod_ref[0, 0:1, :]


def _inproj_kernel(x_ref, mod_ref, nw_ref, w_ref, wdt_ref, o_ref, dt_ref, *, plain_cols, tn):
    hb = _modulated_norm(x_ref[...], nw_ref[...], mod_ref).astype(BF16)
    dt_ref[...] = jnp.dot(hb, wdt_ref[...], preferred_element_type=F32)
    for c0 in range(0, w_ref.shape[1], tn):
        acc = jnp.dot(hb, w_ref[:, c0:c0 + tn], preferred_element_type=F32)
        o_ref[:, c0:c0 + tn] = (acc if c0 < plain_cols else _silu(acc)).astype(BF16)


def _in_proj(x2d, mod3, norm_w, w_main, w_dt, *, tm, n_cols, plain_cols, mod_row):
    m, d = x2d.shape
    const = lambda i: (0, 0)
    once = dict(pipeline_mode=pl.Buffered(1))
    return pl.pallas_call(
        functools.partial(_inproj_kernel, plain_cols=plain_cols, tn=IN_PROJ_TN),
        grid=(m // tm,),
        in_specs=[pl.BlockSpec((tm, d), lambda i: (i, 0)),
                  pl.BlockSpec((1, 3, d), lambda i: (mod_row(i), 0, 0)),
                  pl.BlockSpec((1, d), const),
                  pl.BlockSpec((d, n_cols), const, **once),
                  pl.BlockSpec((d, DT_LANES), const, **once)],
        out_specs=[pl.BlockSpec((tm, n_cols), lambda i: (i, 0)),
                   pl.BlockSpec((tm, DT_LANES), lambda i: (i, 0))],
        out_shape=[jax.ShapeDtypeStruct((m, n_cols), BF16),
                   jax.ShapeDtypeStruct((m, DT_LANES), F32)],
        compiler_params=pltpu.CompilerParams(
            dimension_semantics=("parallel",), vmem_limit_bytes=VMEM_LIMIT_BYTES),
        name="in_proj",
    )(x2d, mod3, norm_w, w_main, w_dt)


N_PIECES = 3
HD = 2 * HPG
LHS_ONES = 0
LHS_LA = 32
ROW_WEND = N_PIECES * HD
ROWS_PER_GROUP = ROW_WEND + HD
MASK_BIG = 1e30
DT_FLOOR = 1e-37
LOG2E = 1.4426950408889634


def _decay_constants():
    k = np.arange(CHUNK)[:, None]
    s = np.arange(CHUNK)[None, :]
    rest = np.zeros((2 * CHUNK, HD * CHUNK), np.float32)
    expand = np.zeros((CHUNK, 2 * GROUP_W), np.float32)
    for h in range(HD):
        cols = slice(h * CHUNK, (h + 1) * CHUNK)
        hidden = (s > k) if h < HPG else (s < k)
        rest[CHUNK:, cols] = np.where(hidden, -MASK_BIG, 0.0)
        for j in range(N_PIECES):
            rest[LHS_LA + HD * j + h, cols] = 1.0
            expand[LHS_LA + HD * j + h, h * HEAD_DIM:(h + 1) * HEAD_DIM] = 1.0
    scatter = np.zeros((4 * DT_LANES, N_GROUPS * CHUNK), np.float32)
    for g in range(N_GROUPS):
        scatter[N_PIECES * DT_LANES, g * CHUNK + LHS_ONES:g * CHUNK + LHS_ONES + ROW_WEND] = 1.0
        for h in range(HD):
            for j in range(N_PIECES):
                scatter[j * DT_LANES + HD * g + h, g * CHUNK + LHS_LA + HD * j + h] = 1.0
    return (jnp.asarray(rest[ROWS_PER_GROUP:], BF16), jnp.asarray(expand, BF16),
            jnp.asarray(np.eye(CHUNK), BF16), jnp.asarray(scatter, BF16))


def _split3(v):
    p0 = v.astype(BF16)
    r1 = v - p0.astype(F32)
    p1 = r1.astype(BF16)
    p2 = (r1 - p1.astype(F32)).astype(BF16)
    return p0, p1, p2


def _dt_kernel(raw_ref, bias_ref, alog_ref, scatter_ref, lhsp_ref, rowpk_ref):
    n_chunks = raw_ref.shape[0] // CHUNK
    bias = bias_ref[...]
    a = -jnp.exp(alog_ref[...])
    row = lax.broadcasted_iota(jnp.int32, (CHUNK, CHUNK), 0)
    col = lax.broadcasted_iota(jnp.int32, (CHUNK, CHUNK), 1)
    tri_lo = (col <= row).astype(BF16)
    tri_up = (col >= row).astype(BF16)
    lane = lax.broadcasted_iota(jnp.int32, (1, DT_LANES), 1)
    is_fwd = (lane % HD) < HPG
    ones = jnp.ones((CHUNK, DT_LANES), BF16)

    def cumulate(tri, pieces):
        acc = jnp.dot(tri, pieces, preferred_element_type=F32)
        return acc[:, :DT_LANES] + acc[:, DT_LANES:2 * DT_LANES] + acc[:, 2 * DT_LANES:]

    for ci in range(n_chunks):
        rows = pl.ds(ci * CHUNK, CHUNK)
        v = raw_ref[rows, :] + bias
        dt = jnp.maximum(v, 0.0) + jnp.log1p(jnp.exp(-jnp.abs(v)))
        dta3 = jnp.concatenate(_split3(dt * a), axis=1)
        la = jnp.where(is_fwd, cumulate(tri_lo, dta3), cumulate(tri_up, dta3))
        la_end = jnp.where(is_fwd, la[CHUNK - 1:CHUNK, :], la[0:1, :])
        wend = jnp.exp(la_end - la) * dt
        la2 = la * LOG2E
        nla = jnp.log2(jnp.maximum(dt, DT_FLOOR)) - la2
        lhs = jnp.concatenate(list(_split3(la2)) + [ones], axis=1)
        lhsp_ref[rows, :] = jnp.dot(lhs, scatter_ref[...],
                                    preferred_element_type=F32).astype(lhsp_ref.dtype)
        pieces = [p.astype(F32) for p in _split3(nla.T)] + [wend.T]
        for g in range(N_GROUPS):
            for j, val in enumerate(pieces):
                rowpk_ref[0, pl.ds(ROWS_PER_GROUP * g + HD * j, HD), rows] = val[HD * g:HD * (g + 1), :]


DT_CHUNKS_PER_STEP = 8


def _dt_prep(dt_raw, bias_p, alog_p, scatter, batch, seq):
    nc = seq // CHUNK
    per = min(DT_CHUNKS_PER_STEP, nc)
    steps = nc // per
    rows = per * CHUNK
    const = lambda b, c: (0, 0)
    return pl.pallas_call(
        _dt_kernel,
        grid=(batch, steps),
        in_specs=[pl.BlockSpec((rows, DT_LANES), lambda b, c: (b * steps + c, 0)),
                  pl.BlockSpec((1, DT_LANES), const),
                  pl.BlockSpec((1, DT_LANES), const),
                  pl.BlockSpec(scatter.shape, const)],
        out_specs=[pl.BlockSpec((rows, N_GROUPS * CHUNK), lambda b, c: (b * steps + c, 0)),
                   pl.BlockSpec((1, N_GROUPS * ROWS_PER_GROUP, rows), lambda b, c: (b, 0, c))],
        out_shape=[jax.ShapeDtypeStruct((batch * seq, N_GROUPS * CHUNK), BF16),
                   jax.ShapeDtypeStruct((batch, N_GROUPS * ROWS_PER_GROUP, seq), F32)],
        compiler_params=pltpu.CompilerParams(dimension_semantics=("parallel", "parallel")),
        name="dt_prep",
    )(dt_raw, bias_p, alog_p, scatter)


def _chunk_start(c):
    return c * CHUNK if isinstance(c, int) else pl.multiple_of(c * CHUNK, CHUNK)


def _dft_conv_matrices(n, in_len, in_cols, out_len, lags, taps_padded):
    half = n // 2
    f = np.arange(half)
    ang = 2.0 * np.pi * np.outer(f, np.arange(in_cols)) / n
    fwd = np.zeros((n, in_cols))
    fwd[:half] = np.cos(ang)
    fwd[half] = np.cos(np.pi * np.arange(in_cols))
    fwd[half + 1:] = -np.sin(ang[1:])
    fwd[:, in_len:] = 0.0
    t = np.arange(out_len)
    angi = 2.0 * np.pi * np.outer(t, f) / n
    inv = np.zeros((out_len, n))
    inv[:, 0] = 1.0 / n
    inv[:, 1:half] = (2.0 / n) * np.cos(angi[:, 1:])
    inv[:, half] = np.cos(np.pi * t) / n
    inv[:, half + 1:] = -(2.0 / n) * np.sin(angi[:, 1:])
    n_taps = len(lags)
    angh = 2.0 * np.pi * np.outer(np.arange(half + 1), np.asarray(lags)) / n
    hre, him = np.cos(angh), -np.sin(angh)
    taps = np.zeros((3 * half, taps_padded))
    taps[:half, :n_taps] = hre[:half]
    taps[half:2 * half, :n_taps] = hre[:half]
    taps[half, :n_taps] = hre[half]
    taps[2 * half + 1:, :n_taps] = him[1:half]
    return jnp.asarray(fwd, F32), jnp.asarray(inv, F32), jnp.asarray(taps, F32)


def _spectrum_product(spec, coef_ref):
    half = spec.shape[0] // 2
    p_re, s_re, q_im = (coef_ref[pl.ds(i * half, half), :] for i in range(3))
    top, bot = spec[:half], spec[half:]
    return jnp.concatenate([top * p_re - bot * q_im, top * q_im + bot * s_re], axis=0)


def _conv_silu_chunk(refs, c, nc, w, b):
    seq = refs[0].shape[0]
    r0 = _chunk_start(c)
    if isinstance(c, int):
        p0, n0 = max(r0 - HALO, 0), min(r0 + CHUNK, seq - HALO)
    else:
        p0 = pl.multiple_of(jnp.maximum(r0 - HALO, 0), HALO)
        n0 = pl.multiple_of(jnp.minimum(r0 + CHUNK, seq - HALO), HALO)

    def rows(start, size):
        return jnp.concatenate([ref[pl.ds(start, size), :] for ref in refs], axis=1).astype(F32)

    prev, main, nxt = rows(p0, HALO), rows(r0, CHUNK), rows(n0, HALO)
    prev = jnp.where(c > 0, prev, 0.0)
    nxt = jnp.where(c < nc - 1, nxt, 0.0)
    ext = jnp.concatenate([prev, main, nxt], axis=0)
    total = CHUNK + 2 * HALO
    xm2 = pltpu.roll(ext, 2, 0)[HALO:HALO + CHUNK]
    xm1 = pltpu.roll(ext, 1, 0)[HALO:HALO + CHUNK]
    xp1 = pltpu.roll(ext, total - 1, 0)[HALO:HALO + CHUNK]
    y = w[0:1] * xm2 + w[1:2] * xm1 + w[2:3] * main + w[3:4] * xp1 + b
    return _silu(y)


def _chunk_operands(lhsp_ref, rowpk_ref, c):
    r0 = _chunk_start(c)
    return lhsp_ref[pl.ds(r0, CHUNK), :], rowpk_ref[0, :, pl.ds(r0, CHUNK)]


def _head_decay(lt, expand_ref, direction=None):
    if direction is None:
        ex = expand_ref[...]
    else:
        ex = expand_ref[:, direction * GROUP_W:(direction + 1) * GROUP_W]
    return jnp.exp2(jnp.dot(lt, ex, preferred_element_type=F32))


def _xs_blockdiag(xs_bf):
    lane = lax.broadcasted_iota(jnp.int32, xs_bf.shape, 1)
    zero = jnp.zeros_like(xs_bf)
    return jnp.concatenate(
        [jnp.where((lane >= r * HEAD_DIM) & (lane < (r + 1) * HEAD_DIM), xs_bf, zero)
         for r in range(HPG)], axis=0)


def _bt_weighted(bt_bf, rp, direction):
    base = ROW_WEND + direction * HPG
    w = rp[base:base + HPG, :].astype(BF16)
    return jnp.concatenate([bt_bf * w[r:r + 1, :] for r in range(HPG)], axis=1)


def _ctx_kernel(xs_ref, b_ref, lhsp_ref, rowpk_ref, cwx_ref, cbx_ref, cwb_ref, cbb_ref,
                expand_ref, o_ref):
    nc = xs_ref.shape[0] // CHUNK
    cw = jnp.concatenate([cwx_ref[...], cwb_ref[...]], axis=1)
    cb = jnp.concatenate([cbx_ref[...], cbb_ref[...]], axis=1)
    contrib, decay = [], []
    for c in range(nc):
        xb = _conv_silu_chunk((xs_ref, b_ref), c, nc, cw, cb)
        xs_bf = xb[:, :GROUP_W].astype(BF16)
        bt = xb[:, GROUP_W:].T.astype(BF16)
        lt, rp = _chunk_operands(lhsp_ref, rowpk_ref, c)
        lhs = jnp.concatenate([_bt_weighted(bt, rp, 0), _bt_weighted(bt, rp, 1)], axis=0)
        contrib.append(jnp.dot(lhs, _xs_blockdiag(xs_bf), preferred_element_type=F32))
        decay.append(_head_decay(lt, expand_ref))
    hf = jnp.zeros((D_STATE, GROUP_W), F32)
    for c in range(nc):
        hf = hf * decay[c][CHUNK - 1:CHUNK, :GROUP_W] + contrib[c][:D_STATE]
    hb = jnp.zeros((D_STATE, GROUP_W), F32)
    for c in reversed(range(nc)):
        hb = hb * decay[c][0:1, GROUP_W:] + contrib[c][D_STATE:]
    o_ref[0, 0, 0] = hf
    o_ref[0, 0, 1] = hb


def _ctx_states(proj_ctx, lhsp, rowpk, conv_w, conv_b, expand, batch, seq):
    bcol = X_END // D_STATE
    const = lambda b, g: (0, 0)
    return pl.pallas_call(
        _ctx_kernel,
        grid=(batch, N_GROUPS),
        in_specs=[pl.BlockSpec((seq, GROUP_W), lambda b, g: (b, g)),
                  pl.BlockSpec((seq, D_STATE), lambda b, g: (b, bcol + g)),
                  pl.BlockSpec((seq, CHUNK), lambda b, g: (b, g)),
                  pl.BlockSpec((1, ROWS_PER_GROUP, seq), lambda b, g: (b, g, 0)),
                  pl.BlockSpec((SSM_CONV, GROUP_W), lambda b, g: (0, g)),
                  pl.BlockSpec((1, GROUP_W), lambda b, g: (0, g)),
                  pl.BlockSpec((SSM_CONV, D_STATE), lambda b, g: (0, bcol + g)),
                  pl.BlockSpec((1, D_STATE), lambda b, g: (0, bcol + g)),
                  pl.BlockSpec(expand.shape, const)],
        out_specs=pl.BlockSpec((1, 1, 2, D_STATE, GROUP_W), lambda b, g: (b, g, 0, 0, 0)),
        out_shape=jax.ShapeDtypeStruct((batch, N_GROUPS, 2, D_STATE, GROUP_W), F32),
        compiler_params=pltpu.CompilerParams(dimension_semantics=("parallel", "parallel")),
        name="ctx_states",
    )(proj_ctx, proj_ctx, lhsp, rowpk, conv_w, conv_b, conv_w, conv_b, expand)


def _ssd_kernel(xs_ref, b_ref, c_ref, z_ref, lhsp_ref, rowpk_ref, cwx_ref, cbx_ref, cwb_ref,
                cbb_ref, cwc_ref, cbc_ref, dsk_ref, nw_ref, h0_ref, rest_ref, expand_ref, eye_ref,
                o_ref, cx_s, cbt_s, cc_s, stash_s, hf_s, hb_s):
    seq = xs_ref.shape[0]
    nc = seq // CHUNK
    cw = jnp.concatenate([cwx_ref[...], cwb_ref[...], cwc_ref[...]], axis=1)
    cb =jnp.concatenate([cbx_ref[...], cbb_ref[...], cbc_ref[...]], axis=1)
    hf_s[...] = h0_ref[0, 0, 0]
    hb_s[...] = h0_ref[0, 0, 1]

    def back_body(i, carry):
        c = nc - 1 - i
        r0 = _chunk_start(c)
        xbc = _conv_silu_chunk((xs_ref, b_ref, c_ref), c, nc, cw, cb)
        xs_bf = xbc[:, :GROUP_W].astype(BF16)
        bt = xbc[:, GROUP_W:GROUP_W + D_STATE].T.astype(BF16)
        cx_s[pl.ds(r0, CHUNK), :] = xs_bf
        cbt_s[pl.ds(r0, CHUNK), :] = bt
        cc_s[pl.ds(r0, CHUNK), :] = xbc[:, GROUP_W + D_STATE:].astype(BF16)
        lt, rp = _chunk_operands(lhsp_ref, rowpk_ref, c)
        hb = hb_s[...]
        stash_s[pl.ds(r0, CHUNK), :] = hb.astype(BF16)
        dec = _head_decay(lt, expand_ref, 1)[0:1, :]
        contrib = jnp.dot(_bt_weighted(bt, rp, 1), _xs_blockdiag(xs_bf), preferred_element_type=F32)
        hb_s[...] = hb * dec + contrib
        return carry

    lax.fori_loop(0, nc, back_body, 0, unroll=4)

    prow = lax.broadcasted_iota(jnp.int32, (ROWS_PER_GROUP, HD * CHUNK), 0)
    pcol = lax.broadcasted_iota(jnp.int32, (ROWS_PER_GROUP, HD * CHUNK), 1)
    own_block = (prow < ROW_WEND) & (prow % HD == pcol // CHUNK)
    dsk, nw = dsk_ref[...], nw_ref[...]

    def fwd_body(c, carry):
        r0 = _chunk_start(c)
        xs_bf = cx_s[pl.ds(r0, CHUNK), :]
        bt = cbt_s[pl.ds(r0, CHUNK), :]
        cc = cc_s[pl.ds(r0, CHUNK), :]
        lt, rp = _chunk_operands(lhsp_ref, rowpk_ref, c)
        scores = jnp.dot(cc, bt, preferred_element_type=F32)
        rows = jnp.concatenate([rp.astype(BF16)] * HD, axis=1)
        rows = jnp.where(own_block, rows, jnp.zeros_like(rows))
        seg = jnp.dot(jnp.concatenate([lt, eye_ref[...]], axis=1),
                      jnp.concatenate([rows, rest_ref[...]], axis=0), preferred_element_type=F32)
        decay_dt = jnp.exp2(seg)
        mix = jnp.concatenate(
            [(scores * (decay_dt[:, r * CHUNK:(r + 1) * CHUNK]
                        + decay_dt[:, (HPG + r) * CHUNK:(HPG + r + 1) * CHUNK])).astype(BF16)
             for r in range(HPG)], axis=1)
        lhs = jnp.concatenate([mix, _bt_weighted(bt, rp, 0)], axis=0)
        prod = jnp.dot(lhs, _xs_blockdiag(xs_bf), preferred_element_type=F32)
        hf = hf_s[...]
        e = _head_decay(lt, expand_ref)
        states = jnp.concatenate([hf.astype(BF16), stash_s[pl.ds(r0, CHUNK), :]], axis=1)
        carried = jnp.dot(cc, states, preferred_element_type=F32) * e
        y = dsk * xs_bf.astype(F32) + prod[:CHUNK] + carried[:, :GROUP_W] + carried[:, GROUP_W:]
        hf_s[...] = hf * e[CHUNK - 1:CHUNK, :GROUP_W] + prod[CHUNK:]
        y = y * z_ref[pl.ds(r0, CHUNK), :].astype(F32)
        ms = jnp.mean(y * y, axis=-1, keepdims=True)
        o_ref[pl.ds(r0, CHUNK), :] = (y * lax.rsqrt(ms + EPS) * nw).astype(o_ref.dtype)
        return carry

    lax.fori_loop(0, nc, fwd_body, 0, unroll=8)


def _ssd(proj, lhsp, rowpk, conv_w, conv_b, dskip_e, norm_w, h0, rest, expand, eye, batch, seq):
    bcol = X_END // D_STATE
    ccol = B_END // D_STATE
    zcol = P_Z // GROUP_W
    gmap = lambda b, g: (0, g)
    const = lambda b, g: (0, 0)
    return pl.pallas_call(
        _ssd_kernel,
        grid=(batch, N_GROUPS),
        in_specs=[pl.BlockSpec((seq, GROUP_W), lambda b, g: (b, g)),
                  pl.BlockSpec((seq, D_STATE), lambda b, g: (b, bcol + g)),
                  pl.BlockSpec((seq, D_STATE), lambda b, g: (b, ccol + g)),
                  pl.BlockSpec((seq, GROUP_W), lambda b, g: (b, zcol + g)),
                  pl.BlockSpec((seq, CHUNK), lambda b, g: (b, g)),
                  pl.BlockSpec((1, ROWS_PER_GROUP, seq), lambda b, g: (b, g, 0)),
                  pl.BlockSpec((SSM_CONV, GROUP_W), gmap),
                  pl.BlockSpec((1, GROUP_W), gmap),
                  pl.BlockSpec((SSM_CONV, D_STATE), lambda b, g: (0, bcol + g)),
                  pl.BlockSpec((1, D_STATE), lambda b, g: (0, bcol + g)),
                  pl.BlockSpec((SSM_CONV, D_STATE), lambda b, g: (0, ccol + g)),
                  pl.BlockSpec((1, D_STATE), lambda b, g: (0, ccol + g)),
                  pl.BlockSpec((1, GROUP_W), gmap),
                  pl.BlockSpec((1, GROUP_W), gmap),
                  pl.BlockSpec((1, 1, 2, D_STATE, GROUP_W), lambda b, g: (b, g, 0, 0, 0)),
                  pl.BlockSpec(rest.shape, const),
                  pl.BlockSpec(expand.shape, const),
                  pl.BlockSpec(eye.shape, const)],
        out_specs=pl.BlockSpec((seq, GROUP_W), lambda b, g: (b, g)),
        out_shape=jax.ShapeDtypeStruct((batch * seq, D_INNER), BF16),
        scratch_shapes=[pltpu.VMEM((seq, GROUP_W), BF16),
                        pltpu.VMEM((seq, D_STATE), BF16),
                        pltpu.VMEM((seq, D_STATE), BF16),
                        pltpu.VMEM((seq, GROUP_W), BF16),
                        pltpu.VMEM((D_STATE, GROUP_W), F32),
                        pltpu.VMEM((D_STATE, GROUP_W), F32)],
        compiler_params=pltpu.CompilerParams(
            dimension_semantics=("parallel", "parallel"), vmem_limit_bytes=VMEM_LIMIT_BYTES),
        name="ssd",
    )(proj, proj, proj, proj, lhsp, rowpk, conv_w, conv_b, conv_w, conv_b, conv_w, conv_b,
      dskip_e, norm_w, h0, rest, expand, eye)


N_TAIL_PROJ = 5
TAIL_TM = 512


DFT_N = 2 * GRID_W
DFT_HALF = DFT_N // 2
CONF_TAPS_PADDED = 32


def _conv_dft_constants():
    lags = [CONF_KERNEL // 2 - k for k in range(CONF_KERNEL)]
    return _dft_conv_matrices(DFT_N, GRID_W, GRID_W, GRID_W, lags, CONF_TAPS_PADDED)


def _tail_kernel(yn_ref, x_ref, xnext_ref, mod_ref, modnext_ref, nw_ref, wt_ref, cw_ref, cb_ref,
                 fwd_ref, inv_ref, taps_ref, lnw_ref, lnb_ref, wos_ref, woc_ref, wo_ref, fnw_ref,
                 o_ref, ub_s, uc_s, proj_s, act_s, coef_s):
    tm = x_ref.shape[0]
    nseq = tm // GRID_W

    def project(xv, mref):
        h = _modulated_norm(xv, nw_ref[...], mref).astype(BF16)
        for t in range(N_TAIL_PROJ):
            proj_s[t] = jnp.dot(h, wt_ref[t], preferred_element_type=F32)

    @pl.when(pl.program_id(0) == 0)
    def _():
        project(x_ref[...], mod_ref)
        coef_s[...] = jnp.dot(taps_ref[...], cw_ref[...], preferred_element_type=F32,
                              precision=lax.Precision.HIGHEST)

    ub_s[...] = (proj_s[0] * _sigmoid(proj_s[1])).astype(BF16)
    act_s[0] = _silu(proj_s[2])
    act_s[1] = _sigmoid(proj_s[3])
    act_s[2] = _sigmoid(proj_s[4])
    project(xnext_ref[...], modnext_ref)

    for q in range(nseq):
        rows = pl.ds(q * GRID_W, GRID_W)
        spec = jnp.dot(fwd_ref[...], ub_s[rows, :], preferred_element_type=F32)
        prod = _spectrum_product(spec, coef_s).astype(BF16)
        uc_s[rows, :] = jnp.dot(inv_ref[...], prod, preferred_element_type=F32) + cb_ref[...]
    x = x_ref[...]
    uc = uc_s[...]
    mu = jnp.mean(uc, axis=-1, keepdims=True)
    dev = uc - mu
    var = jnp.mean(dev * dev, axis=-1, keepdims=True)
    ln = dev * lax.rsqrt(var + EPS) * lnw_ref[...] + lnb_ref[...]
    u2 = _silu(ln) * act_s[0]
    branch_conf = jnp.dot(u2.astype(BF16), woc_ref[...], preferred_element_type=F32)
    branch_ssm = jnp.dot(yn_ref[...], wos_ref[...], preferred_element_type=F32)
    merged = act_s[1] * branch_ssm + act_s[2] * branch_conf
    out = jnp.dot(merged.astype(BF16), wo_ref[...], preferred_element_type=F32)
    xn = x + mod_ref[0, 2:3, :] * out
    ms = jnp.mean(xn * xn, axis=-1, keepdims=True)
    o_ref[...] = xn * lax.rsqrt(ms + EPS) * fnw_ref[...]


def _tail(yn, x2d, mod3, norm_w, w_tail, conf_w, conf_b, ln_w, ln_b, w_os, w_oc, w_o, fn_w, *,
          tm, seq):
    m, d = x2d.shape
    tiles_per_batch = seq // tm
    const = lambda i: (0, 0)
    fwd, inv, taps = _conv_dft_constants()
    n_tiles = m // tm
    nxt = lambda i: jnp.minimum(i + 1, n_tiles - 1)
    once = dict(pipeline_mode=pl.Buffered(1))
    return pl.pallas_call(
        _tail_kernel,
        grid=(m // tm,),
        in_specs=[pl.BlockSpec((tm, D_INNER), lambda i: (i, 0)),
                  pl.BlockSpec((tm, d), lambda i: (i, 0)),
                  pl.BlockSpec((tm, d), lambda i: (nxt(i), 0)),
                  pl.BlockSpec((1, 3, d), lambda i: (i // tiles_per_batch, 0, 0)),
                  pl.BlockSpec((1, 3, d), lambda i: (nxt(i) // tiles_per_batch, 0, 0)),
                  pl.BlockSpec((1, d), const),
                  pl.BlockSpec(w_tail.shape, lambda i: (0, 0, 0), **once),
                  pl.BlockSpec((CONF_TAPS_PADDED, D_CONF), const),
                  pl.BlockSpec((1, D_CONF), const),
                  pl.BlockSpec(fwd.shape, const),
                  pl.BlockSpec(inv.shape, const),
                  pl.BlockSpec(taps.shape, const),
                  pl.BlockSpec((1, D_CONF), const),
                  pl.BlockSpec((1, D_CONF), const),
                  pl.BlockSpec((D_INNER, D_MODEL), const, **once),
                  pl.BlockSpec((D_CONF, D_MODEL), const, **once),
                  pl.BlockSpec((D_MODEL, D_MODEL), const, **once),
                  pl.BlockSpec((1, D_MODEL), const)],
        out_specs=pl.BlockSpec((tm, d), lambda i: (i, 0)),
        out_shape=jax.ShapeDtypeStruct((m, d), F32),
        scratch_shapes=[pltpu.VMEM((tm, D_CONF), BF16),
                        pltpu.VMEM((tm, D_CONF), F32),
                        pltpu.VMEM((N_TAIL_PROJ, tm, D_CONF), F32),
                        pltpu.VMEM((3, tm, D_CONF), F32),
                        pltpu.VMEM((3 * DFT_HALF, D_CONF), F32)],
        compiler_params=pltpu.CompilerParams(
            dimension_semantics=("arbitrary",), vmem_limit_bytes=VMEM_LIMIT_BYTES),
        name="tail",
    )(yn, x2d, x2d, mod3, mod3, norm_w, w_tail, conf_w, conf_b, fwd.astype(BF16),
      inv.astype(BF16), taps, ln_w, ln_b, w_os, w_oc, w_o, fn_w)


def _head_perm():
    return np.array([d * N_HEADS + g * HPG + r
                     for g in range(N_GROUPS) for d in range(2) for r in range(HPG)])


def kernel(x, c, ctx, c_ctx, w_mod, b_mod, norm_w, w_in, ssm_conv_w, ssm_conv_b, dt_bias, a_log,
           d_skip, ssm_norm_w, w_out_ssm, conf_conv_w, conf_conv_b, conf_ln_w, conf_ln_b,
           w_out_conf, w_out, final_norm_w):
    batch, seq, d = x.shape
    ctx_len = ctx.shape[1]
    assert w_in.shape[0] == 1, "single trunk layer"
    assert d == D_MODEL and seq % CHUNK == 0 and ctx_len % CHUNK == 0 and batch + 1 <= 8

    w_in0 = w_in[0]
    w_bf = w_in0.astype(BF16)
    w_main = jnp.concatenate([w_bf[:, :C_END], w_bf[:, DT_END:Z_END]], axis=1)
    w_tail = w_bf[:, Z_END:].reshape(d, N_TAIL_PROJ, D_CONF).transpose(1, 0, 2)
    perm = _head_perm()
    pad_heads = DT_LANES - 2 * N_HEADS
    w_dt = jnp.pad(w_in0[:, C_END:DT_END][:, perm], ((0, 0), (0, pad_heads))).astype(BF16)
    bias_p = jnp.pad(dt_bias[0].reshape(-1)[perm], (0, pad_heads)).reshape(1, DT_LANES)
    alog_p = jnp.pad(a_log[0].reshape(-1)[perm], (0, pad_heads)).reshape(1, DT_LANES)
    conv_w = ssm_conv_w[0]
    conv_b = ssm_conv_b[0].reshape(1, C_END)
    dskip_e = jnp.repeat(d_skip[0], HEAD_DIM).reshape(1, D_INNER)
    nw2 = norm_w[0].reshape(1, d)
    rest, expand, eye, scatter = _decay_constants()

    c_all = jnp.concatenate([c, c_ctx[None, :], jnp.zeros((8 - batch - 1, d), F32)], axis=0)
    mod3 = _mod(c_all, w_mod[0], b_mod[0]).reshape(8, 3, d)

    ctx_rows = batch * ctx_len
    tm_ctx = min(IN_PROJ_TM, ctx_rows)
    proj_ctx, dtraw_ctx = _in_proj(ctx.reshape(ctx_rows, d), mod3, nw2, w_main, w_dt,
                                   tm=tm_ctx, n_cols=B_END, plain_cols=B_END,
                                   mod_row=lambda i: batch)
    lhsp_ctx, rowpk_ctx = _dt_prep(dtraw_ctx, bias_p, alog_p, scatter, batch, ctx_len)
    h0 = _ctx_states(proj_ctx, lhsp_ctx, rowpk_ctx, conv_w, conv_b, expand, batch, ctx_len)

    m = batch * seq
    x2d = x.reshape(m, d)
    tm = min(IN_PROJ_TM, seq)
    tiles_per_batch = seq // tm
    proj, dtraw = _in_proj(x2d, mod3, nw2, w_main, w_dt, tm=tm, n_cols=P_COLS, plain_cols=C_END,
                           mod_row=lambda i: i // tiles_per_batch)
    lhsp, rowpk = _dt_prep(dtraw, bias_p, alog_p, scatter, batch, seq)
    yn = _ssd(proj, lhsp, rowpk, conv_w, conv_b, dskip_e, ssm_norm_w[0].reshape(1, D_INNER), h0,
              rest, expand, eye, batch, seq)
    conf_w = jnp.pad(conf_conv_w[0], ((0, CONF_TAPS_PADDED - CONF_KERNEL), (0, 0)))
    out = _tail(yn, x2d, mod3, nw2, w_tail, conf_w, conf_conv_b[0].reshape(1, D_CONF),
                conf_ln_w[0].reshape(1, D_CONF), conf_ln_b[0].reshape(1, D_CONF),
                w_out_ssm[0].astype(BF16), w_out_conf[0].astype(BF16), w_out[0].astype(BF16),
                final_norm_w.reshape(1, d), tm=min(TAIL_TM, seq), seq=seq)
    return out.reshape(batch, seq, d)
```

```python
import functools

import numpy as np
import jax
import jax.numpy as jnp
from jax import lax
from jax.experimental import pallas as pl
from jax.experimental.pallas import tpu as pltpu

F32 = jnp.float32
BF16 = jnp.bfloat16

D_MODEL = 1024
GRID_W = 64
D_INNER = 2 * D_MODEL
HEAD_DIM = 64
N_HEADS = D_INNER // HEAD_DIM
N_GROUPS = 8
HPG = N_HEADS // N_GROUPS
D_STATE = 128
SSM_CONV = 4
CHUNK = 128
D_CONF = D_MODEL
CONF_KERNEL = 31
EPS = 1e-6

GN = N_GROUPS * D_STATE
X_END = D_INNER
B_END = X_END + GN
C_END = B_END + GN
DT_END = C_END + 2 * N_HEADS
Z_END = DT_END + D_INNER
GLU_END = Z_END + 2 * D_CONF
CG_END = GLU_END + D_CONF
IN_COLS = CG_END + 2 * D_MODEL

GROUP_W = HPG * HEAD_DIM
DT_LANES = 128
P_Z = C_END
P_COLS = C_END + D_INNER
IN_PROJ_TN = 1024
IN_PROJ_TM = 1024

HALO = 16
VMEM_LIMIT_BYTES = 58 * 1024 * 1024


def _sigmoid(v):
    return 1.0 / (1.0 + jnp.exp(-v))


def _silu(v):
    half = 0.5 * v
    return half + half * jnp.tanh(half)


def _mod_kernel(c_ref, w_ref, b_ref, o_ref):
    s = _silu(c_ref[...])
    o_ref[...] = jnp.dot(s, w_ref[...], preferred_element_type=F32,
                         precision=lax.Precision.HIGHEST) + b_ref[...]


def _mod(c_all, w_mod, b_mod):
    rows, d = c_all.shape
    n = w_mod.shape[1]
    tn = 1024
    return pl.pallas_call(
        _mod_kernel,
        grid=(n // tn,),
        in_specs=[pl.BlockSpec((rows, d), lambda j: (0, 0)),
                  pl.BlockSpec((d, tn), lambda j: (0, j)),
                  pl.BlockSpec((1, tn), lambda j: (0, j))],
        out_specs=pl.BlockSpec((rows, tn), lambda j: (0, j)),
        out_shape=jax.ShapeDtypeStruct((rows, n), F32),
        name="mod",
    )(c_all, w_mod, b_mod.reshape(1, n))


def _modulated_norm(x, nw, mod_ref):
    ms = jnp.mean(x * x, axis=-1, keepdims=True)
    y = x * lax.rsqrt(ms + EPS) * nw
    return y * (1.0 + mod_ref[0, 1:2, :]) + mod_ref[0, 0:1, :]


def _inproj_kernel(x_ref, mod_ref, nw_ref, w_ref, wdt_ref, o_ref, dt_ref, *, plain_cols, tn):
    hb = _modulated_norm(x_ref[...], nw_ref[...], mod_ref).astype(BF16)
    dt_ref[...] = jnp.dot(hb, wdt_ref[...], preferred_element_type=F32)
    for c0 in range(0, w_ref.shape[1], tn):
        acc = jnp.dot(hb, w_ref[:, c0:c0 + tn], preferred_element_type=F32)
        o_ref[:, c0:c0 + tn] = (acc if c0 < plain_cols else _silu(acc)).astype(BF16)


def _in_proj(x2d, mod3, norm_w, w_main, w_dt, *, tm, n_cols, plain_cols, mod_row):
    m, d = x2d.shape
    const = lambda i: (0, 0)
    once = dict(pipeline_mode=pl.Buffered(1))
    return pl.pallas_call(
        functools.partial(_inproj_kernel, plain_cols=plain_cols, tn=IN_PROJ_TN),
        grid=(m // tm,),
        in_specs=[pl.BlockSpec((tm, d), lambda i: (i, 0)),
                  pl.BlockSpec((1, 3, d), lambda i: (mod_row(i), 0, 0)),
                  pl.BlockSpec((1, d), const),
                  pl.BlockSpec((d, n_cols), const, **once),
                  pl.BlockSpec((d, DT_LANES), const, **once)],
        out_specs=[pl.BlockSpec((tm, n_cols), lambda i: (i, 0)),
                   pl.BlockSpec((tm, DT_LANES), lambda i: (i, 0))],
        out_shape=[jax.ShapeDtypeStruct((m, n_cols), BF16),
                   jax.ShapeDtypeStruct((m, DT_LANES), F32)],
        compiler_params=pltpu.CompilerParams(
            dimension_semantics=("parallel",), vmem_limit_bytes=VMEM_LIMIT_BYTES),
        name="in_proj",
    )(x2d, mod3, norm_w, w_main, w_dt)


N_PIECES = 3
HD = 2 * HPG
LHS_ONES = 0
LHS_LA = 32
ROW_WEND = N_PIECES * HD
ROWS_PER_GROUP = ROW_WEND + HD
MASK_BIG = 1e30
DT_FLOOR = 1e-37
LOG2E = 1.4426950408889634


def _decay_constants():
    k = np.arange(CHUNK)[:, None]
    s = np.arange(CHUNK)[None, :]
    rest = np.zeros((2 * CHUNK, HD * CHUNK), np.float32)
    expand = np.zeros((CHUNK, 2 * GROUP_W), np.float32)
    for h in range(HD):
        cols = slice(h * CHUNK, (h + 1) * CHUNK)
        hidden = (s > k) if h < HPG else (s < k)
        rest[CHUNK:, cols] = np.where(hidden, -MASK_BIG, 0.0)
        for j in range(N_PIECES):
            rest[LHS_LA + HD * j + h, cols] = 1.0
            expand[LHS_LA + HD * j + h, h * HEAD_DIM:(h + 1) * HEAD_DIM] = 1.0
    scatter = np.zeros((4 * DT_LANES, N_GROUPS * CHUNK), np.float32)
    for g in range(N_GROUPS):
        scatter[N_PIECES * DT_LANES, g * CHUNK + LHS_ONES:g * CHUNK + LHS_ONES + ROW_WEND] = 1.0
        for h in range(HD):
            for j in range(N_PIECES):
                scatter[j * DT_LANES + HD * g + h, g * CHUNK + LHS_LA + HD * j + h] = 1.0
    return (jnp.asarray(rest[ROWS_PER_GROUP:], BF16), jnp.asarray(expand, BF16),
            jnp.asarray(np.eye(CHUNK), BF16), jnp.asarray(scatter, BF16))


def _split3(v):
    p0 = v.astype(BF16)
    r1 = v - p0.astype(F32)
    p1 = r1.astype(BF16)
    p2 = (r1 - p1.astype(F32)).astype(BF16)
    return p0, p1, p2


def _dt_kernel(raw_ref, bias_ref, alog_ref, scatter_ref, lhsp_ref, rowpk_ref):
    n_chunks = raw_ref.shape[0] // CHUNK
    bias = bias_ref[...]
    a = -jnp.exp(alog_ref[...])
    row = lax.broadcasted_iota(jnp.int32, (CHUNK, CHUNK), 0)
    col = lax.broadcasted_iota(jnp.int32, (CHUNK, CHUNK), 1)
    tri_lo = (col <= row).astype(BF16)
    tri_up = (col >= row).astype(BF16)
    lane = lax.broadcasted_iota(jnp.int32, (1, DT_LANES), 1)
    is_fwd = (lane % HD) < HPG
    ones = jnp.ones((CHUNK, DT_LANES), BF16)

    def cumulate(tri, pieces):
        acc = jnp.dot(tri, pieces, preferred_element_type=F32)
        return acc[:, :DT_LANES] + acc[:, DT_LANES:2 * DT_LANES] + acc[:, 2 * DT_LANES:]

    for ci in range(n_chunks):
        rows = pl.ds(ci * CHUNK, CHUNK)
        v = raw_ref[rows, :] + bias
        dt = jnp.maximum(v, 0.0) + jnp.log1p(jnp.exp(-jnp.abs(v)))
        dta3 = jnp.concatenate(_split3(dt * a), axis=1)
        la = jnp.where(is_fwd, cumulate(tri_lo, dta3), cumulate(tri_up, dta3))
        la_end = jnp.where(is_fwd, la[CHUNK - 1:CHUNK, :], la[0:1, :])
        wend = jnp.exp(la_end - la) * dt
        la2 = la * LOG2E
        nla = jnp.log2(jnp.maximum(dt, DT_FLOOR)) - la2
        lhs = jnp.concatenate(list(_split3(la2)) + [ones], axis=1)
        lhsp_ref[rows, :] = jnp.dot(lhs, scatter_ref[...],
                                    preferred_element_type=F32).astype(lhsp_ref.dtype)
        pieces = [p.astype(F32) for p in _split3(nla.T)] + [wend.T]
        for g in range(N_GROUPS):
            for j, val in enumerate(pieces):
                rowpk_ref[0, pl.ds(ROWS_PER_GROUP * g + HD * j, HD), rows] = val[HD * g:HD * (g + 1), :]


DT_CHUNKS_PER_STEP = 8


def _dt_prep(dt_raw, bias_p, alog_p, scatter, batch, seq):
    nc = seq // CHUNK
    per = min(DT_CHUNKS_PER_STEP, nc)
    steps = nc // per
    rows = per * CHUNK
    const = lambda b, c: (0, 0)
    return pl.pallas_call(
        _dt_kernel,
        grid=(batch, steps),
        in_specs=[pl.BlockSpec((rows, DT_LANES), lambda b, c: (b * steps + c, 0)),
                  pl.BlockSpec((1, DT_LANES), const),
                  pl.BlockSpec((1, DT_LANES), const),
                  pl.BlockSpec(scatter.shape, const)],
        out_specs=[pl.BlockSpec((rows, N_GROUPS * CHUNK), lambda b, c: (b * steps + c, 0)),
                   pl.BlockSpec((1, N_GROUPS * ROWS_PER_GROUP, rows), lambda b, c: (b, 0, c))],
        out_shape=[jax.ShapeDtypeStruct((batch * seq, N_GROUPS * CHUNK), BF16),
                   jax.ShapeDtypeStruct((batch, N_GROUPS * ROWS_PER_GROUP, seq), F32)],
        compiler_params=pltpu.CompilerParams(dimension_semantics=("parallel", "parallel")),
        name="dt_prep",
    )(dt_raw, bias_p, alog_p, scatter)


def _chunk_start(c):
    return c * CHUNK if isinstance(c, int) else pl.multiple_of(c * CHUNK, CHUNK)


def _dft_conv_matrices(n, in_len, in_cols, out_len, lags, taps_padded):
    half = n // 2
    f = np.arange(half)
    ang = 2.0 * np.pi * np.outer(f, np.arange(in_cols)) / n
    fwd = np.zeros((n, in_cols))
    fwd[:half] = np.cos(ang)
    fwd[half] = np.cos(np.pi * np.arange(in_cols))
    fwd[half + 1:] = -np.sin(ang[1:])
    fwd[:, in_len:] = 0.0
    t = np.arange(out_len)
    angi = 2.0 * np.pi * np.outer(t, f) / n
    inv = np.zeros((out_len, n))
    inv[:, 0] = 1.0 / n
    inv[:, 1:half] = (2.0 / n) * np.cos(angi[:, 1:])
    inv[:, half] = np.cos(np.pi * t) / n
    inv[:, half + 1:] = -(2.0 / n) * np.sin(angi[:, 1:])
    n_taps = len(lags)
    angh = 2.0 * np.pi * np.outer(np.arange(half + 1), np.asarray(lags)) / n
    hre, him = np.cos(angh), -np.sin(angh)
    taps = np.zeros((3 * half, taps_padded))
    taps[:half, :n_taps] = hre[:half]
    taps[half:2 * half, :n_taps] = hre[:half]
    taps[half, :n_taps] = hre[half]
    taps[2 * half + 1:, :n_taps] = him[1:half]
    return jnp.asarray(fwd, F32), jnp.asarray(inv, F32), jnp.asarray(taps, F32)


def _spectrum_product(spec, coef_ref):
    half = spec.shape[0] // 2
    p_re, s_re, q_im = (coef_ref[pl.ds(i * half, half), :] for i in range(3))
    top, bot = spec[:half], spec[half:]
    return jnp.concatenate([top * p_re - bot * q_im, top * q_im + bot * s_re], axis=0)


def _conv_silu_chunk(refs, c, nc, w, b):
    seq = refs[0].shape[0]
    r0 = _chunk_start(c)
    if isinstance(c, int):
        p0, n0 = max(r0 - HALO, 0), min(r0 + CHUNK, seq - HALO)
    else:
        p0 = pl.multiple_of(jnp.maximum(r0 - HALO, 0), HALO)
        n0 = pl.multiple_of(jnp.minimum(r0 + CHUNK, seq - HALO), HALO)

    def rows(start, size):
        return jnp.concatenate([ref[pl.ds(start, size), :] for ref in refs], axis=1).astype(F32)

    prev, main, nxt = rows(p0, HALO), rows(r0, CHUNK), rows(n0, HALO)
    prev = jnp.where(c > 0, prev, 0.0)
    nxt = jnp.where(c < nc - 1, nxt, 0.0)
    ext = jnp.concatenate([prev, main, nxt], axis=0)
    total = CHUNK + 2 * HALO
    xm2 = pltpu.roll(ext, 2, 0)[HALO:HALO + CHUNK]
    xm1 = pltpu.roll(ext, 1, 0)[HALO:HALO + CHUNK]
    xp1 = pltpu.roll(ext, total - 1, 0)[HALO:HALO + CHUNK]
    y = w[0:1] * xm2 + w[1:2] * xm1 + w[2:3] * main + w[3:4] * xp1 + b
    return _silu(y)


def _chunk_operands(lhsp_ref, rowpk_ref, c):
    r0 = _chunk_start(c)
    return lhsp_ref[pl.ds(r0, CHUNK), :], rowpk_ref[0, :, pl.ds(r0, CHUNK)]


def _head_decay(lt, expand_ref, direction=None):
    if direction is None:
        ex = expand_ref[...]
    else:
        ex = expand_ref[:, direction * GROUP_W:(direction + 1) * GROUP_W]
    return jnp.exp2(jnp.dot(lt, ex, preferred_element_type=F32))


def _xs_blockdiag(xs_bf):
    lane = lax.broadcasted_iota(jnp.int32, xs_bf.shape, 1)
    zero = jnp.zeros_like(xs_bf)
    return jnp.concatenate(
        [jnp.where((lane >= r * HEAD_DIM) & (lane < (r + 1) * HEAD_DIM), xs_bf, zero)
         for r in range(HPG)], axis=0)


def _bt_weighted(bt_bf, rp, direction):
    base = ROW_WEND + direction * HPG
    w = rp[base:base + HPG, :].astype(BF16)
    return jnp.concatenate([bt_bf * w[r:r + 1, :] for r in range(HPG)], axis=1)


def _ctx_kernel(xs_ref, b_ref, lhsp_ref, rowpk_ref, cwx_ref, cbx_ref, cwb_ref, cbb_ref,
                expand_ref, o_ref):
    nc = xs_ref.shape[0] // CHUNK
    cw = jnp.concatenate([cwx_ref[...], cwb_ref[...]], axis=1)
    cb = jnp.concatenate([cbx_ref[...], cbb_ref[...]], axis=1)
    contrib, decay = [], []
    for c in range(nc):
        xb = _conv_silu_chunk((xs_ref, b_ref), c, nc, cw, cb)
        xs_bf = xb[:, :GROUP_W].astype(BF16)
        bt = xb[:, GROUP_W:].T.astype(BF16)
        lt, rp = _chunk_operands(lhsp_ref, rowpk_ref, c)
        lhs = jnp.concatenate([_bt_weighted(bt, rp, 0), _bt_weighted(bt, rp, 1)], axis=0)
        contrib.append(jnp.dot(lhs, _xs_blockdiag(xs_bf), preferred_element_type=F32))
        decay.append(_head_decay(lt, expand_ref))
    hf = jnp.zeros((D_STATE, GROUP_W), F32)
    for c in range(nc):
        hf = hf * decay[c][CHUNK - 1:CHUNK, :GROUP_W] + contrib[c][:D_STATE]
    hb = jnp.zeros((D_STATE, GROUP_W), F32)
    for c in reversed(range(nc)):
        hb = hb * decay[c][0:1, GROUP_W:] + contrib[c][D_STATE:]
    o_ref[0, 0, 0] = hf
    o_ref[0, 0, 1] = hb


def _ctx_states(proj_ctx, lhsp, rowpk, conv_w, conv_b, expand, batch, seq):
    bcol = X_END // D_STATE
    const = lambda b, g: (0, 0)
    return pl.pallas_call(
        _ctx_kernel,
        grid=(batch, N_GROUPS),
        in_specs=[pl.BlockSpec((seq, GROUP_W), lambda b, g: (b, g)),
                  pl.BlockSpec((seq, D_STATE), lambda b, g: (b, bcol + g)),
                  pl.BlockSpec((seq, CHUNK), lambda b, g: (b, g)),
                  pl.BlockSpec((1, ROWS_PER_GROUP, seq), lambda b, g: (b, g, 0)),
                  pl.BlockSpec((SSM_CONV, GROUP_W), lambda b, g: (0, g)),
                  pl.BlockSpec((1, GROUP_W), lambda b, g: (0, g)),
                  pl.BlockSpec((SSM_CONV, D_STATE), lambda b, g: (0, bcol + g)),
                  pl.BlockSpec((1, D_STATE), lambda b, g: (0, bcol + g)),
                  pl.BlockSpec(expand.shape, const)],
        out_specs=pl.BlockSpec((1, 1, 2, D_STATE, GROUP_W), lambda b, g: (b, g, 0, 0, 0)),
        out_shape=jax.ShapeDtypeStruct((batch, N_GROUPS, 2, D_STATE, GROUP_W), F32),
        compiler_params=pltpu.CompilerParams(dimension_semantics=("parallel", "parallel")),
        name="ctx_states",
    )(proj_ctx, proj_ctx, lhsp, rowpk, conv_w, conv_b, conv_w, conv_b, expand)


def _ssd_kernel(xs_ref, b_ref, c_ref, z_ref, lhsp_ref, rowpk_ref, cwx_ref, cbx_ref, cwb_ref,
                cbb_ref, cwc_ref, cbc_ref, dsk_ref, nw_ref, h0_ref, rest_ref, expand_ref, eye_ref,
                o_ref, cx_s, cbt_s, cc_s, stash_s, hf_s, hb_s):
    seq = xs_ref.shape[0]
    nc = seq // CHUNK
    cw = jnp.concatenate([cwx_ref[...], cwb_ref[...], cwc_ref[...]], axis=1)
    cb =jnp.concatenate([cbx_ref[...], cbb_ref[...], cbc_ref[...]], axis=1)
    hf_s[...] = h0_ref[0, 0, 0]
    hb_s[...] = h0_ref[0, 0, 1]

    def back_body(i, carry):
        c = nc - 1 - i
        r0 = _chunk_start(c)
        xbc = _conv_silu_chunk((xs_ref, b_ref, c_ref), c, nc, cw, cb)
        xs_bf = xbc[:, :GROUP_W].astype(BF16)
        bt = xbc[:, GROUP_W:GROUP_W + D_STATE].T.astype(BF16)
        cx_s[pl.ds(r0, CHUNK), :] = xs_bf
        cbt_s[pl.ds(r0, CHUNK), :] = bt
        cc_s[pl.ds(r0, CHUNK), :] = xbc[:, GROUP_W + D_STATE:].astype(BF16)
        lt, rp = _chunk_operands(lhsp_ref, rowpk_ref, c)
        hb = hb_s[...]
        stash_s[pl.ds(r0, CHUNK), :] = hb.astype(BF16)
        dec = _head_decay(lt, expand_ref, 1)[0:1, :]
        contrib = jnp.dot(_bt_weighted(bt, rp, 1), _xs_blockdiag(xs_bf), preferred_element_type=F32)
        hb_s[...] = hb * dec + contrib
        return carry

    lax.fori_loop(0, nc, back_body, 0, unroll=4)

    prow = lax.broadcasted_iota(jnp.int32, (ROWS_PER_GROUP, HD * CHUNK), 0)
    pcol = lax.broadcasted_iota(jnp.int32, (ROWS_PER_GROUP, HD * CHUNK), 1)
    own_block = (prow < ROW_WEND) & (prow % HD == pcol // CHUNK)
    dsk, nw = dsk_ref[...], nw_ref[...]

    def fwd_body(c, carry):
        r0 = _chunk_start(c)
        xs_bf = cx_s[pl.ds(r0, CHUNK), :]
        bt = cbt_s[pl.ds(r0, CHUNK), :]
        cc = cc_s[pl.ds(r0, CHUNK), :]
        lt, rp = _chunk_operands(lhsp_ref, rowpk_ref, c)
        scores = jnp.dot(cc, bt, preferred_element_type=F32)
        rows = jnp.concatenate([rp.astype(BF16)] * HD, axis=1)
        rows = jnp.where(own_block, rows, jnp.zeros_like(rows))
        seg = jnp.dot(jnp.concatenate([lt, eye_ref[...]], axis=1),
                      jnp.concatenate([rows, rest_ref[...]], axis=0), preferred_element_type=F32)
        decay_dt = jnp.exp2(seg)
        mix = jnp.concatenate(
            [(scores * (decay_dt[:, r * CHUNK:(r + 1) * CHUNK]
                        + decay_dt[:, (HPG + r) * CHUNK:(HPG + r + 1) * CHUNK])).astype(BF16)
             for r in range(HPG)], axis=1)
        lhs = jnp.concatenate([mix, _bt_weighted(bt, rp, 0)], axis=0)
        prod = jnp.dot(lhs, _xs_blockdiag(xs_bf), preferred_element_type=F32)
        hf = hf_s[...]
        e = _head_decay(lt, expand_ref)
        states = jnp.concatenate([hf.astype(BF16), stash_s[pl.ds(r0, CHUNK), :]], axis=1)
        carried = jnp.dot(cc, states, preferred_element_type=F32) * e
        y = dsk * xs_bf.astype(F32) + prod[:CHUNK] + carried[:, :GROUP_W] + carried[:, GROUP_W:]
        hf_s[...] = hf * e[CHUNK - 1:CHUNK, :GROUP_W] + prod[CHUNK:]
        y = y * z_ref[pl.ds(r0, CHUNK), :].astype(F32)
        ms = jnp.mean(y * y, axis=-1, keepdims=True)
        o_ref[pl.ds(r0, CHUNK), :] = (y * lax.rsqrt(ms + EPS) * nw).astype(o_ref.dtype)
        return carry

    lax.fori_loop(0, nc, fwd_body, 0, unroll=8)


def _ssd(proj, lhsp, rowpk, conv_w, conv_b, dskip_e, norm_w, h0, rest, expand, eye, batch, seq):
    bcol = X_END // D_STATE
    ccol = B_END // D_STATE
    zcol = P_Z // GROUP_W
    gmap = lambda b, g: (0, g)
    const = lambda b, g: (0, 0)
    return pl.pallas_call(
        _ssd_kernel,
        grid=(batch, N_GROUPS),
        in_specs=[pl.BlockSpec((seq, GROUP_W), lambda b, g: (b, g)),
                  pl.BlockSpec((seq, D_STATE), lambda b, g: (b, bcol + g)),
                  pl.BlockSpec((seq, D_STATE), lambda b, g: (b, ccol + g)),
                  pl.BlockSpec((seq, GROUP_W), lambda b, g: (b, zcol + g)),
                  pl.BlockSpec((seq, CHUNK), lambda b, g: (b, g)),
                  pl.BlockSpec((1, ROWS_PER_GROUP, seq), lambda b, g: (b, g, 0)),
                  pl.BlockSpec((SSM_CONV, GROUP_W), gmap),
                  pl.BlockSpec((1, GROUP_W), gmap),
                  pl.BlockSpec((SSM_CONV, D_STATE), lambda b, g: (0, bcol + g)),
                  pl.BlockSpec((1, D_STATE), lambda b, g: (0, bcol + g)),
                  pl.BlockSpec((SSM_CONV, D_STATE), lambda b, g: (0, ccol + g)),
                  pl.BlockSpec((1, D_STATE), lambda b, g: (0, ccol + g)),
                  pl.BlockSpec((1, GROUP_W), gmap),
                  pl.BlockSpec((1, GROUP_W), gmap),
                  pl.BlockSpec((1, 1, 2, D_STATE, GROUP_W), lambda b, g: (b, g, 0, 0, 0)),
                  pl.BlockSpec(rest.shape, const),
                  pl.BlockSpec(expand.shape, const),
                  pl.BlockSpec(eye.shape, const)],
        out_specs=pl.BlockSpec((seq, GROUP_W), lambda b, g: (b, g)),
        out_shape=jax.ShapeDtypeStruct((batch * seq, D_INNER), BF16),
        scratch_shapes=[pltpu.VMEM((seq, GROUP_W), BF16),
                        pltpu.VMEM((seq, D_STATE), BF16),
                        pltpu.VMEM((seq, D_STATE), BF16),
                        pltpu.VMEM((seq, GROUP_W), BF16),
                        pltpu.VMEM((D_STATE, GROUP_W), F32),
                        pltpu.VMEM((D_STATE, GROUP_W), F32)],
        compiler_params=pltpu.CompilerParams(
            dimension_semantics=("parallel", "parallel"), vmem_limit_bytes=VMEM_LIMIT_BYTES),
        name="ssd",
    )(proj, proj, proj, proj, lhsp, rowpk, conv_w, conv_b, conv_w, conv_b, conv_w, conv_b,
      dskip_e, norm_w, h0, rest, expand, eye)


N_TAIL_PROJ = 5
TAIL_TM = 512


DFT_N = 2 * GRID_W
DFT_HALF = DFT_N // 2
CONF_TAPS_PADDED = 32


def _conv_dft_constants():
    lags = [CONF_KERNEL // 2 - k for k in range(CONF_KERNEL)]
    return _dft_conv_matrices(DFT_N, GRID_W, GRID_W, GRID_W, lags, CONF_TAPS_PADDED)


def _tail_kernel(yn_ref, x_ref, xnext_ref, mod_ref, modnext_ref, nw_ref, wt0_ref, wt1_ref, wt2_ref,
                 wt3_ref, wt4_ref, cw_ref, cb_ref, fwd_ref, inv_ref, taps_ref, lnw_ref, lnb_ref,
                 wos_ref, woc_ref, wo_ref, fnw_ref, o_ref, ub_s, uc_s, proj_s, act_s, coef_s):
    tm = x_ref.shape[0]
    nseq = tm // GRID_W
    wt_refs = (wt0_ref, wt1_ref, wt2_ref, wt3_ref, wt4_ref)

    def project(xv, mref):
        h = _modulated_norm(xv, nw_ref[...], mref).astype(BF16)
        for t in range(N_TAIL_PROJ):
            proj_s[t] = jnp.dot(h, wt_refs[t][...], preferred_element_type=F32)

    @pl.when(pl.program_id(0) == 0)
    def _():
        project(x_ref[...], mod_ref)
        coef_s[...] = jnp.dot(taps_ref[...], cw_ref[...], preferred_element_type=F32,
                              precision=lax.Precision.HIGHEST)

    ub_s[...] = (proj_s[0] * _sigmoid(proj_s[1])).astype(BF16)
    act_s[0] = _silu(proj_s[2])
    act_s[1] = _sigmoid(proj_s[3])
    act_s[2] = _sigmoid(proj_s[4])
    project(xnext_ref[...], modnext_ref)

    for q in range(nseq):
        rows = pl.ds(q * GRID_W, GRID_W)
        spec = jnp.dot(fwd_ref[...], ub_s[rows, :], preferred_element_type=F32)
        prod = _spectrum_product(spec, coef_s).astype(BF16)
        uc_s[rows, :] = jnp.dot(inv_ref[...], prod, preferred_element_type=F32) + cb_ref[...]
    x = x_ref[...]
    uc = uc_s[...]
    mu = jnp.mean(uc, axis=-1, keepdims=True)
    dev = uc - mu
    var = jnp.mean(dev * dev, axis=-1, keepdims=True)
    ln = dev * lax.rsqrt(var + EPS) * lnw_ref[...] + lnb_ref[...]
    u2 = _silu(ln) * act_s[0]
    branch_conf = jnp.dot(u2.astype(BF16), woc_ref[...], preferred_element_type=F32)
    branch_ssm = jnp.dot(yn_ref[...], wos_ref[...], preferred_element_type=F32)
    merged = act_s[1] * branch_ssm + act_s[2] * branch_conf
    out = jnp.dot(merged.astype(BF16), wo_ref[...], preferred_element_type=F32)
    xn = x + mod_ref[0, 2:3, :] * out
    ms = jnp.mean(xn * xn, axis=-1, keepdims=True)
    o_ref[...] = xn * lax.rsqrt(ms + EPS) * fnw_ref[...]


def _tail(yn, x2d, mod3, norm_w, w_all, conf_w, conf_b, ln_w, ln_b, w_os, w_oc, w_o, fn_w, *,
          tm, seq):
    m, d = x2d.shape
    tiles_per_batch = seq // tm
    const = lambda i: (0, 0)
    fwd, inv, taps = _conv_dft_constants()
    n_tiles = m // tm
    nxt = lambda i: jnp.minimum(i + 1, n_tiles - 1)
    once = dict(pipeline_mode=pl.Buffered(1))
    return pl.pallas_call(
        _tail_kernel,
        grid=(m // tm,),
        in_specs=[pl.BlockSpec((tm, D_INNER), lambda i: (i, 0)),
                  pl.BlockSpec((tm, d), lambda i: (i, 0)),
                  pl.BlockSpec((tm, d), lambda i: (nxt(i), 0)),
                  pl.BlockSpec((1, 3, d), lambda i: (i // tiles_per_batch, 0, 0)),
                  pl.BlockSpec((1, 3, d), lambda i: (nxt(i) // tiles_per_batch, 0, 0)),
                  pl.BlockSpec((1, d), const),
                  *[pl.BlockSpec((d, D_CONF), functools.partial(lambda t, i: (0, t), P_COLS // D_CONF + t),
                                 **once) for t in range(N_TAIL_PROJ)],
                  pl.BlockSpec((CONF_TAPS_PADDED, D_CONF), const),
                  pl.BlockSpec((1, D_CONF), const),
                  pl.BlockSpec(fwd.shape, const),
                  pl.BlockSpec(inv.shape, const),
                  pl.BlockSpec(taps.shape, const),
                  pl.BlockSpec((1, D_CONF), const),
                  pl.BlockSpec((1, D_CONF), const),
                  pl.BlockSpec((D_INNER, D_MODEL), const, **once),
                  pl.BlockSpec((D_CONF, D_MODEL), const, **once),
                  pl.BlockSpec((D_MODEL, D_MODEL), const, **once),
                  pl.BlockSpec((1, D_MODEL), const)],
        out_specs=pl.BlockSpec((tm, d), lambda i: (i, 0)),
        out_shape=jax.ShapeDtypeStruct((m, d), F32),
        scratch_shapes=[pltpu.VMEM((tm, D_CONF), BF16),
                        pltpu.VMEM((tm, D_CONF), F32),
                        pltpu.VMEM((N_TAIL_PROJ, tm, D_CONF), F32),
                        pltpu.VMEM((3, tm, D_CONF), F32),
                        pltpu.VMEM((3 * DFT_HALF, D_CONF), F32)],
        compiler_params=pltpu.CompilerParams(
            dimension_semantics=("arbitrary",), vmem_limit_bytes=VMEM_LIMIT_BYTES),
        name="tail",
    )(yn, x2d, x2d, mod3, mod3, norm_w, *([w_all] * N_TAIL_PROJ), conf_w, conf_b, fwd.astype(BF16),
      inv.astype(BF16), taps, ln_w, ln_b, w_os, w_oc, w_o, fn_w)


def _head_perm():
    return np.array([d * N_HEADS + g * HPG + r
                     for g in range(N_GROUPS) for d in range(2) for r in range(HPG)])


def kernel(x, c, ctx, c_ctx, w_mod, b_mod, norm_w, w_in, ssm_conv_w, ssm_conv_b, dt_bias, a_log,
           d_skip, ssm_norm_w, w_out_ssm, conf_conv_w, conf_conv_b, conf_ln_w, conf_ln_b,
           w_out_conf, w_out, final_norm_w):
    batch, seq, d = x.shape
    ctx_len = ctx.shape[1]
    assert w_in.shape[0] == 1, "single trunk layer"
    assert d == D_MODEL and seq % CHUNK == 0 and ctx_len % CHUNK == 0 and batch + 1 <= 8

    w_in0 = w_in[0]
    w_all = jnp.concatenate([w_in0[:, :C_END], w_in0[:, DT_END:]], axis=1).astype(BF16)
    perm = _head_perm()
    pad_heads = DT_LANES - 2 * N_HEADS
    w_dt = jnp.pad(w_in0[:, C_END:DT_END][:, perm], ((0, 0), (0, pad_heads))).astype(BF16)
    bias_p = jnp.pad(dt_bias[0].reshape(-1)[perm], (0, pad_heads)).reshape(1, DT_LANES)
    alog_p = jnp.pad(a_log[0].reshape(-1)[perm], (0, pad_heads)).reshape(1, DT_LANES)
    conv_w = ssm_conv_w[0]
    conv_b = ssm_conv_b[0].reshape(1, C_END)
    dskip_e = jnp.repeat(d_skip[0], HEAD_DIM).reshape(1, D_INNER)
    nw2 = norm_w[0].reshape(1, d)
    rest, expand, eye, scatter = _decay_constants()

    c_all = jnp.concatenate([c, c_ctx[None, :], jnp.zeros((8 - batch - 1, d), F32)], axis=0)
    mod3 = _mod(c_all, w_mod[0], b_mod[0]).reshape(8, 3, d)

    ctx_rows = batch * ctx_len
    tm_ctx = min(IN_PROJ_TM, ctx_rows)
    proj_ctx, dtraw_ctx = _in_proj(ctx.reshape(ctx_rows, d), mod3, nw2, w_all, w_dt,
                                   tm=tm_ctx, n_cols=B_END, plain_cols=B_END,
                                   mod_row=lambda i: batch)
    lhsp_ctx, rowpk_ctx = _dt_prep(dtraw_ctx, bias_p, alog_p, scatter, batch, ctx_len)
    h0 = _ctx_states(proj_ctx, lhsp_ctx, rowpk_ctx, conv_w, conv_b, expand, batch, ctx_len)

    m = batch * seq
    x2d = x.reshape(m, d)
    tm = min(IN_PROJ_TM, seq)
    tiles_per_batch = seq // tm
    proj, dtraw = _in_proj(x2d, mod3, nw2, w_all, w_dt, tm=tm, n_cols=P_COLS, plain_cols=C_END,
                           mod_row=lambda i: i // tiles_per_batch)
    lhsp, rowpk = _dt_prep(dtraw, bias_p, alog_p, scatter, batch, seq)
    yn = _ssd(proj, lhsp, rowpk, conv_w, conv_b, dskip_e, ssm_norm_w[0].reshape(1, D_INNER), h0,
              rest, expand, eye, batch, seq)
    conf_w = jnp.pad(conf_conv_w[0], ((0, CONF_TAPS_PADDED - CONF_KERNEL), (0, 0)))
    out = _tail(yn, x2d, mod3, nw2, w_all, conf_w, conf_conv_b[0].reshape(1, D_CONF),
                conf_ln_w[0].reshape(1, D_CONF), conf_ln_b[0].reshape(1, D_CONF),
                w_out_ssm[0].astype(BF16), w_out_conf[0].astype(BF16), w_out[0].astype(BF16),
                final_norm_w.reshape(1, d), tm=min(TAIL_TM, seq), seq=seq)
    return out.reshape(batch, seq, d)
```

```python
import functools

import numpy as np
import jax
import jax.numpy as jnp
from jax import lax
from jax.experimental import pallas as pl
from jax.experimental.pallas import tpu as pltpu

F32 = jnp.float32
BF16 = jnp.bfloat16

D_MODEL = 1024
GRID_W = 64
D_INNER = 2 * D_MODEL
HEAD_DIM = 64
N_HEADS = D_INNER // HEAD_DIM
N_GROUPS = 8
HPG = N_HEADS // N_GROUPS
D_STATE = 128
SSM_CONV = 4
CHUNK = 128
D_CONF = D_MODEL
CONF_KERNEL = 31
EPS = 1e-6

GN = N_GROUPS * D_STATE
X_END = D_INNER
B_END = X_END + GN
C_END = B_END + GN
DT_END = C_END + 2 * N_HEADS
Z_END = DT_END + D_INNER
GLU_END = Z_END + 2 * D_CONF
CG_END = GLU_END + D_CONF
IN_COLS = CG_END + 2 * D_MODEL

GROUP_W = HPG * HEAD_DIM
DT_LANES = 128
P_Z = C_END
P_COLS = C_END + D_INNER
IN_PROJ_TN = 1024
IN_PROJ_TM = 1024

HALO = 16
VMEM_LIMIT_BYTES = 58 * 1024 * 1024


def _sigmoid(v):
    return 1.0 / (1.0 + jnp.exp(-v))


def _silu(v):
    half = 0.5 * v
    return half + half * jnp.tanh(half)


def _mod_kernel(c_ref, w_ref, b_ref, o_ref):
    s = _silu(c_ref[...])
    o_ref[...] = jnp.dot(s, w_ref[...], preferred_element_type=F32,
                         precision=lax.Precision.HIGHEST) + b_ref[...]


def _mod(c_all, w_mod, b_mod):
    rows, d = c_all.shape
    n = w_mod.shape[1]
    tn = 1024
    return pl.pallas_call(
        _mod_kernel,
        grid=(n // tn,),
        in_specs=[pl.BlockSpec((rows, d), lambda j: (0, 0)),
                  pl.BlockSpec((d, tn), lambda j: (0, j)),
                  pl.BlockSpec((1, tn), lambda j: (0, j))],
        out_specs=pl.BlockSpec((rows, tn), lambda j: (0, j)),
        out_shape=jax.ShapeDtypeStruct((rows, n), F32),
        name="mod",
    )(c_all, w_mod, b_mod.reshape(1, n))


def _modulated_norm(x, nw, mod_ref):
    ms = jnp.mean(x * x, axis=-1, keepdims=True)
    y = x * lax.rsqrt(ms + EPS) * nw
    return y * (1.0 + mod_ref[0, 1:2, :]) + mod_ref[0, 0:1, :]


def _inproj_kernel(x_ref, mod_ref, nw_ref, w_ref, wdt_ref, o_ref, dt_ref, *, plain_cols, tn):
    hb = _modulated_norm(x_ref[...], nw_ref[...], mod_ref).astype(BF16)
    dt_ref[...] = jnp.dot(hb, wdt_ref[...], preferred_element_type=F32)
    for c0 in range(0, w_ref.shape[1], tn):
        acc = jnp.dot(hb, w_ref[:, c0:c0 + tn], preferred_element_type=F32)
        o_ref[:, c0:c0 + tn] = (acc if c0 < plain_cols else _silu(acc)).astype(BF16)


def _in_proj(x2d, mod3, norm_w, w_main, w_dt, *, tm, n_cols, plain_cols, mod_row):
    m, d = x2d.shape
    const = lambda i: (0, 0)
    once = dict(pipeline_mode=pl.Buffered(1))
    return pl.pallas_call(
        functools.partial(_inproj_kernel, plain_cols=plain_cols, tn=IN_PROJ_TN),
        grid=(m // tm,),
        in_specs=[pl.BlockSpec((tm, d), lambda i: (i, 0)),
                  pl.BlockSpec((1, 3, d), lambda i: (mod_row(i), 0, 0)),
                  pl.BlockSpec((1, d), const),
                  pl.BlockSpec((d, n_cols), const, **once),
                  pl.BlockSpec((d, DT_LANES), const, **once)],
        out_specs=[pl.BlockSpec((tm, n_cols), lambda i: (i, 0)),
                   pl.BlockSpec((tm, DT_LANES), lambda i: (i, 0))],
        out_shape=[jax.ShapeDtypeStruct((m, n_cols), BF16),
                   jax.ShapeDtypeStruct((m, DT_LANES), F32)],
        compiler_params=pltpu.CompilerParams(
            dimension_semantics=("parallel",), vmem_limit_bytes=VMEM_LIMIT_BYTES),
        name="in_proj",
    )(x2d, mod3, norm_w, w_main, w_dt)


N_PIECES = 3
HD = 2 * HPG
LHS_ONES = 0
LHS_LA = 32
ROW_WEND = N_PIECES * HD
ROWS_PER_GROUP = ROW_WEND + HD
MASK_BIG = 1e30
DT_FLOOR = 1e-37
LOG2E = 1.4426950408889634


def _decay_constants():
    k = np.arange(CHUNK)[:, None]
    s = np.arange(CHUNK)[None, :]
    rest = np.zeros((2 * CHUNK, HD * CHUNK), np.float32)
    expand = np.zeros((CHUNK, 2 * GROUP_W), np.float32)
    for h in range(HD):
        cols = slice(h * CHUNK, (h + 1) * CHUNK)
        hidden = (s > k) if h < HPG else (s < k)
        rest[CHUNK:, cols] = np.where(hidden, -MASK_BIG, 0.0)
        for j in range(N_PIECES):
            rest[LHS_LA + HD * j + h, cols] = 1.0
            expand[LHS_LA + HD * j + h, h * HEAD_DIM:(h + 1) * HEAD_DIM] = 1.0
    scatter = np.zeros((4 * DT_LANES, N_GROUPS * CHUNK), np.float32)
    for g in range(N_GROUPS):
        scatter[N_PIECES * DT_LANES, g * CHUNK + LHS_ONES:g * CHUNK + LHS_ONES + ROW_WEND] = 1.0
        for h in range(HD):
            for j in range(N_PIECES):
                scatter[j * DT_LANES + HD * g + h, g * CHUNK + LHS_LA + HD * j + h] = 1.0
    return (jnp.asarray(rest[ROWS_PER_GROUP:], BF16), jnp.asarray(expand, BF16),
            jnp.asarray(np.eye(CHUNK), BF16), jnp.asarray(scatter, BF16))


def _split3(v):
    p0 = v.astype(BF16)
    r1 = v - p0.astype(F32)
    p1 = r1.astype(BF16)
    p2 = (r1 - p1.astype(F32)).astype(BF16)
    return p0, p1, p2


def _dt_kernel(raw_ref, bias_ref, alog_ref, scatter_ref, lhsp_ref, rowpk_ref):
    n_chunks = raw_ref.shape[0] // CHUNK
    bias = bias_ref[...]
    a = -jnp.exp(alog_ref[...])
    row = lax.broadcasted_iota(jnp.int32, (CHUNK, CHUNK), 0)
    col = lax.broadcasted_iota(jnp.int32, (CHUNK, CHUNK), 1)
    tri_lo = (col <= row).astype(BF16)
    tri_up = (col >= row).astype(BF16)
    lane = lax.broadcasted_iota(jnp.int32, (1, DT_LANES), 1)
    is_fwd = (lane % HD) < HPG
    ones = jnp.ones((CHUNK, DT_LANES), BF16)

    def cumulate(tri, pieces):
        acc = jnp.dot(tri, pieces, preferred_element_type=F32)
        return acc[:, :DT_LANES] + acc[:, DT_LANES:2 * DT_LANES] + acc[:, 2 * DT_LANES:]

    for ci in range(n_chunks):
        rows = pl.ds(ci * CHUNK, CHUNK)
        v = raw_ref[rows, :] + bias
        dt = jnp.maximum(v, 0.0) + jnp.log1p(jnp.exp(-jnp.abs(v)))
        dta3 = jnp.concatenate(_split3(dt * a), axis=1)
        la = jnp.where(is_fwd, cumulate(tri_lo, dta3), cumulate(tri_up, dta3))
        la_end = jnp.where(is_fwd, la[CHUNK - 1:CHUNK, :], la[0:1, :])
        wend = jnp.exp(la_end - la) * dt
        la2 = la * LOG2E
        nla = jnp.log2(jnp.maximum(dt, DT_FLOOR)) - la2
        lhs = jnp.concatenate(list(_split3(la2)) + [ones], axis=1)
        lhsp_ref[rows, :] = jnp.dot(lhs, scatter_ref[...],
                                    preferred_element_type=F32).astype(lhsp_ref.dtype)
        pieces = [p.astype(F32) for p in _split3(nla.T)] + [wend.T]
        for g in range(N_GROUPS):
            for j, val in enumerate(pieces):
                rowpk_ref[0, pl.ds(ROWS_PER_GROUP * g + HD * j, HD), rows] = val[HD * g:HD * (g + 1), :]


DT_CHUNKS_PER_STEP = 8


def _dt_prep(dt_raw, bias_p, alog_p, scatter, batch, seq):
    nc = seq // CHUNK
    per = min(DT_CHUNKS_PER_STEP, nc)
    steps = nc // per
    rows = per * CHUNK
    const = lambda b, c: (0, 0)
    return pl.pallas_call(
        _dt_kernel,
        grid=(batch, steps),
        in_specs=[pl.BlockSpec((rows, DT_LANES), lambda b, c: (b * steps + c, 0)),
                  pl.BlockSpec((1, DT_LANES), const),
                  pl.BlockSpec((1, DT_LANES), const),
                  pl.BlockSpec(scatter.shape, const)],
        out_specs=[pl.BlockSpec((rows, N_GROUPS * CHUNK), lambda b, c: (b * steps + c, 0)),
                   pl.BlockSpec((1, N_GROUPS * ROWS_PER_GROUP, rows), lambda b, c: (b, 0, c))],
        out_shape=[jax.ShapeDtypeStruct((batch * seq, N_GROUPS * CHUNK), BF16),
                   jax.ShapeDtypeStruct((batch, N_GROUPS * ROWS_PER_GROUP, seq), F32)],
        compiler_params=pltpu.CompilerParams(dimension_semantics=("parallel", "parallel")),
        name="dt_prep",
    )(dt_raw, bias_p, alog_p, scatter)


def _chunk_start(c):
    return c * CHUNK if isinstance(c, int) else pl.multiple_of(c * CHUNK, CHUNK)


def _dft_conv_matrices(n, in_len, in_cols, out_len, lags, taps_padded):
    half = n // 2
    f = np.arange(half)
    ang = 2.0 * np.pi * np.outer(f, np.arange(in_cols)) / n
    fwd = np.zeros((n, in_cols))
    fwd[:half] = np.cos(ang)
    fwd[half] = np.cos(np.pi * np.arange(in_cols))
    fwd[half + 1:] = -np.sin(ang[1:])
    fwd[:, in_len:] = 0.0
    t = np.arange(out_len)
    angi = 2.0 * np.pi * np.outer(t, f) / n
    inv = np.zeros((out_len, n))
    inv[:, 0] = 1.0 / n
    inv[:, 1:half] = (2.0 / n) * np.cos(angi[:, 1:])
    inv[:, half] = np.cos(np.pi * t) / n
    inv[:, half + 1:] = -(2.0 / n) * np.sin(angi[:, 1:])
    n_taps = len(lags)
    angh = 2.0 * np.pi * np.outer(np.arange(half + 1), np.asarray(lags)) / n
    hre, him = np.cos(angh), -np.sin(angh)
    taps = np.zeros((3 * half, taps_padded))
    taps[:half, :n_taps] = hre[:half]
    taps[half:2 * half, :n_taps] = hre[:half]
    taps[half, :n_taps] = hre[half]
    taps[2 * half + 1:, :n_taps] = him[1:half]
    return jnp.asarray(fwd, F32), jnp.asarray(inv, F32), jnp.asarray(taps, F32)


def _spectrum_product(spec, coef_ref):
    half = spec.shape[0] // 2
    p_re, s_re, q_im = (coef_ref[pl.ds(i * half, half), :] for i in range(3))
    top, bot = spec[:half], spec[half:]
    return jnp.concatenate([top * p_re - bot * q_im, top * q_im + bot * s_re], axis=0)


def _conv_silu_chunk(refs, c, nc, w, b):
    seq = refs[0].shape[0]
    r0 = _chunk_start(c)
    if isinstance(c, int):
        p0, n0 = max(r0 - HALO, 0), min(r0 + CHUNK, seq - HALO)
    else:
        p0 = pl.multiple_of(jnp.maximum(r0 - HALO, 0), HALO)
        n0 = pl.multiple_of(jnp.minimum(r0 + CHUNK, seq - HALO), HALO)

    def rows(start, size):
        return jnp.concatenate([ref[pl.ds(start, size), :] for ref in refs], axis=1).astype(F32)

    prev, main, nxt = rows(p0, HALO), rows(r0, CHUNK), rows(n0, HALO)
    prev = jnp.where(c > 0, prev, 0.0)
    nxt = jnp.where(c < nc - 1, nxt, 0.0)
    ext = jnp.concatenate([prev, main, nxt], axis=0)
    total = CHUNK + 2 * HALO
    xm2 = pltpu.roll(ext, 2, 0)[HALO:HALO + CHUNK]
    xm1 = pltpu.roll(ext, 1, 0)[HALO:HALO + CHUNK]
    xp1 = pltpu.roll(ext, total - 1, 0)[HALO:HALO + CHUNK]
    y = w[0:1] * xm2 + w[1:2] * xm1 + w[2:3] * main + w[3:4] * xp1 + b
    return _silu(y)


def _chunk_operands(lhsp_ref, rowpk_ref, c):
    r0 = _chunk_start(c)
    return lhsp_ref[pl.ds(r0, CHUNK), :], rowpk_ref[0, :, pl.ds(r0, CHUNK)]


def _head_decay(lt, expand_ref, direction=None):
    if direction is None:
        ex = expand_ref[...]
    else:
        ex = expand_ref[:, direction * GROUP_W:(direction + 1) * GROUP_W]
    return jnp.exp2(jnp.dot(lt, ex, preferred_element_type=F32))


def _xs_blockdiag(xs_bf):
    lane = lax.broadcasted_iota(jnp.int32, xs_bf.shape, 1)
    zero = jnp.zeros_like(xs_bf)
    return jnp.concatenate(
        [jnp.where((lane >= r * HEAD_DIM) & (lane < (r + 1) * HEAD_DIM), xs_bf, zero)
         for r in range(HPG)], axis=0)


def _bt_weighted(bt_bf, rp, direction):
    base = ROW_WEND + direction * HPG
    w = rp[base:base + HPG, :].astype(BF16)
    return jnp.concatenate([bt_bf * w[r:r + 1, :] for r in range(HPG)], axis=1)


def _ctx_kernel(xs_ref, b_ref, lhsp_ref, rowpk_ref, cwx_ref, cbx_ref, cwb_ref, cbb_ref,
                expand_ref, o_ref):
    nc = xs_ref.shape[0] // CHUNK
    cw = jnp.concatenate([cwx_ref[...], cwb_ref[...]], axis=1)
    cb = jnp.concatenate([cbx_ref[...], cbb_ref[...]], axis=1)
    contrib, decay = [], []
    for c in range(nc):
        xb = _conv_silu_chunk((xs_ref, b_ref), c, nc, cw, cb)
        xs_bf = xb[:, :GROUP_W].astype(BF16)
        bt = xb[:, GROUP_W:].T.astype(BF16)
        lt, rp = _chunk_operands(lhsp_ref, rowpk_ref, c)
        lhs = jnp.concatenate([_bt_weighted(bt, rp, 0), _bt_weighted(bt, rp, 1)], axis=0)
        contrib.append(jnp.dot(lhs, _xs_blockdiag(xs_bf), preferred_element_type=F32))
        decay.append(_head_decay(lt, expand_ref))
    hf = jnp.zeros((D_STATE, GROUP_W), F32)
    for c in range(nc):
        hf = hf * decay[c][CHUNK - 1:CHUNK, :GROUP_W] + contrib[c][:D_STATE]
    hb = jnp.zeros((D_STATE, GROUP_W), F32)
    for c in reversed(range(nc)):
        hb = hb * decay[c][0:1, GROUP_W:] + contrib[c][D_STATE:]
    o_ref[0, 0, 0] = hf
    o_ref[0, 0, 1] = hb


def _ctx_states(proj_ctx, lhsp, rowpk, conv_w, conv_b, expand, batch, seq):
    bcol = X_END // D_STATE
    const = lambda b, g: (0, 0)
    return pl.pallas_call(
        _ctx_kernel,
        grid=(batch, N_GROUPS),
        in_specs=[pl.BlockSpec((seq, GROUP_W), lambda b, g: (b, g)),
                  pl.BlockSpec((seq, D_STATE), lambda b, g: (b, bcol + g)),
                  pl.BlockSpec((seq, CHUNK), lambda b, g: (b, g)),
                  pl.BlockSpec((1, ROWS_PER_GROUP, seq), lambda b, g: (b, g, 0)),
                  pl.BlockSpec((SSM_CONV, GROUP_W), lambda b, g: (0, g)),
                  pl.BlockSpec((1, GROUP_W), lambda b, g: (0, g)),
                  pl.BlockSpec((SSM_CONV, D_STATE), lambda b, g: (0, bcol + g)),
                  pl.BlockSpec((1, D_STATE), lambda b, g: (0, bcol + g)),
                  pl.BlockSpec(expand.shape, const)],
        out_specs=pl.BlockSpec((1, 1, 2, D_STATE, GROUP_W), lambda b, g: (b, g, 0, 0, 0)),
        out_shape=jax.ShapeDtypeStruct((batch, N_GROUPS, 2, D_STATE, GROUP_W), F32),
        compiler_params=pltpu.CompilerParams(dimension_semantics=("parallel", "parallel")),
        name="ctx_states",
    )(proj_ctx, proj_ctx, lhsp, rowpk, conv_w, conv_b, conv_w, conv_b, expand)


def _ssd_kernel(xs_ref, b_ref, c_ref, z_ref, lhsp_ref, rowpk_ref, cwx_ref, cbx_ref, cwb_ref,
                cbb_ref, cwc_ref, cbc_ref, dsk_ref, nw_ref, h0_ref, rest_ref, expand_ref, eye_ref,
                o_ref, cx_s, cbt_s, cc_s, stash_s, hf_s, hb_s):
    seq = xs_ref.shape[0]
    nc = seq // CHUNK
    cw = jnp.concatenate([cwx_ref[...], cwb_ref[...], cwc_ref[...]], axis=1)
    cb =jnp.concatenate([cbx_ref[...], cbb_ref[...], cbc_ref[...]], axis=1)
    hf_s[...] = h0_ref[0, 0, 0]
    hb_s[...] = h0_ref[0, 0, 1]

    def back_body(i, carry):
        c = nc - 1 - i
        r0 = _chunk_start(c)
        xbc = _conv_silu_chunk((xs_ref, b_ref, c_ref), c, nc, cw, cb)
        xs_bf = xbc[:, :GROUP_W].astype(BF16)
        bt = xbc[:, GROUP_W:GROUP_W + D_STATE].T.astype(BF16)
        cx_s[pl.ds(r0, CHUNK), :] = xs_bf
        cbt_s[pl.ds(r0, CHUNK), :] = bt
        cc_s[pl.ds(r0, CHUNK), :] = xbc[:, GROUP_W + D_STATE:].astype(BF16)
        lt, rp = _chunk_operands(lhsp_ref, rowpk_ref, c)
        hb = hb_s[...]
        stash_s[pl.ds(r0, CHUNK), :] = hb.astype(BF16)
        dec = _head_decay(lt, expand_ref, 1)[0:1, :]
        contrib = jnp.dot(_bt_weighted(bt, rp, 1), _xs_blockdiag(xs_bf), preferred_element_type=F32)
        hb_s[...] = hb * dec + contrib
        return carry

    lax.fori_loop(0, nc, back_body, 0, unroll=8)

    prow = lax.broadcasted_iota(jnp.int32, (ROWS_PER_GROUP, HD * CHUNK), 0)
    pcol = lax.broadcasted_iota(jnp.int32, (ROWS_PER_GROUP, HD * CHUNK), 1)
    own_block = (prow < ROW_WEND) & (prow % HD == pcol // CHUNK)
    dsk, nw = dsk_ref[...], nw_ref[...]

    def fwd_body(c, carry):
        r0 = _chunk_start(c)
        xs_bf = cx_s[pl.ds(r0, CHUNK), :]
        bt = cbt_s[pl.ds(r0, CHUNK), :]
        cc = cc_s[pl.ds(r0, CHUNK), :]
        lt, rp = _chunk_operands(lhsp_ref, rowpk_ref, c)
        scores = jnp.dot(cc, bt, preferred_element_type=F32)
        rows = jnp.concatenate([rp.astype(BF16)] * HD, axis=1)
        rows = jnp.where(own_block, rows, jnp.zeros_like(rows))
        seg = jnp.dot(jnp.concatenate([lt, eye_ref[...]], axis=1),
                      jnp.concatenate([rows, rest_ref[...]], axis=0), preferred_element_type=F32)
        decay_dt = jnp.exp2(seg)
        mix = jnp.concatenate(
            [(scores * (decay_dt[:, r * CHUNK:(r + 1) * CHUNK]
                        + decay_dt[:, (HPG + r) * CHUNK:(HPG + r + 1) * CHUNK])).astype(BF16)
             for r in range(HPG)], axis=1)
        lhs = jnp.concatenate([mix, _bt_weighted(bt, rp, 0)], axis=0)
        prod = jnp.dot(lhs, _xs_blockdiag(xs_bf), preferred_element_type=F32)
        hf = hf_s[...]
        e = _head_decay(lt, expand_ref)
        states = jnp.concatenate([hf.astype(BF16), stash_s[pl.ds(r0, CHUNK), :]], axis=1)
        carried = jnp.dot(cc, states, preferred_element_type=F32) * e
        y = dsk * xs_bf.astype(F32) + prod[:CHUNK] + carried[:, :GROUP_W] + carried[:, GROUP_W:]
        hf_s[...] = hf * e[CHUNK - 1:CHUNK, :GROUP_W] + prod[CHUNK:]
        y = y * z_ref[pl.ds(r0, CHUNK), :].astype(F32)
        ms = jnp.mean(y * y, axis=-1, keepdims=True)
        o_ref[pl.ds(r0, CHUNK), :] = (y * lax.rsqrt(ms + EPS) * nw).astype(o_ref.dtype)
        return carry

    lax.fori_loop(0, nc, fwd_body, 0, unroll=16)


def _ssd(proj, lhsp, rowpk, conv_w, conv_b, dskip_e, norm_w, h0, rest, expand, eye, batch, seq):
    bcol = X_END // D_STATE
    ccol = B_END // D_STATE
    zcol = P_Z // GROUP_W
    gmap = lambda b, g: (0, g)
    const = lambda b, g: (0, 0)
    return pl.pallas_call(
        _ssd_kernel,
        grid=(batch, N_GROUPS),
        in_specs=[pl.BlockSpec((seq, GROUP_W), lambda b, g: (b, g)),
                  pl.BlockSpec((seq, D_STATE), lambda b, g: (b, bcol + g)),
                  pl.BlockSpec((seq, D_STATE), lambda b, g: (b, ccol + g)),
                  pl.BlockSpec((seq, GROUP_W), lambda b, g: (b, zcol + g)),
                  pl.BlockSpec((seq, CHUNK), lambda b, g: (b, g)),
                  pl.BlockSpec((1, ROWS_PER_GROUP, seq), lambda b, g: (b, g, 0)),
                  pl.BlockSpec((SSM_CONV, GROUP_W), gmap),
                  pl.BlockSpec((1, GROUP_W), gmap),
                  pl.BlockSpec((SSM_CONV, D_STATE), lambda b, g: (0, bcol + g)),
                  pl.BlockSpec((1, D_STATE), lambda b, g: (0, bcol + g)),
                  pl.BlockSpec((SSM_CONV, D_STATE), lambda b, g: (0, ccol + g)),
                  pl.BlockSpec((1, D_STATE), lambda b, g: (0, ccol + g)),
                  pl.BlockSpec((1, GROUP_W), gmap),
                  pl.BlockSpec((1, GROUP_W), gmap),
                  pl.BlockSpec((1, 1, 2, D_STATE, GROUP_W), lambda b, g: (b, g, 0, 0, 0)),
                  pl.BlockSpec(rest.shape, const),
                  pl.BlockSpec(expand.shape, const),
                  pl.BlockSpec(eye.shape, const)],
        out_specs=pl.BlockSpec((seq, GROUP_W), lambda b, g: (b, g)),
        out_shape=jax.ShapeDtypeStruct((batch * seq, D_INNER), BF16),
        scratch_shapes=[pltpu.VMEM((seq, GROUP_W), BF16),
                        pltpu.VMEM((seq, D_STATE), BF16),
                        pltpu.VMEM((seq, D_STATE), BF16),
                        pltpu.VMEM((seq, GROUP_W), BF16),
                        pltpu.VMEM((D_STATE, GROUP_W), F32),
                        pltpu.VMEM((D_STATE, GROUP_W), F32)],
        compiler_params=pltpu.CompilerParams(
            dimension_semantics=("parallel", "parallel"), vmem_limit_bytes=VMEM_LIMIT_BYTES),
        name="ssd",
    )(proj, proj, proj, proj, lhsp, rowpk, conv_w, conv_b, conv_w, conv_b, conv_w, conv_b,
      dskip_e, norm_w, h0, rest, expand, eye)


N_TAIL_PROJ = 5
TAIL_TM = 512


DFT_N = 2 * GRID_W
DFT_HALF = DFT_N // 2
CONF_TAPS_PADDED = 32


def _conv_dft_constants():
    lags = [CONF_KERNEL // 2 - k for k in range(CONF_KERNEL)]
    return _dft_conv_matrices(DFT_N, GRID_W, GRID_W, GRID_W, lags, CONF_TAPS_PADDED)


def _tail_kernel(yn_ref, x_ref, xnext_ref, mod_ref, modnext_ref, nw_ref, wt0_ref, wt1_ref, wt2_ref,
                 wt3_ref, wt4_ref, cw_ref, cb_ref, fwd_ref, inv_ref, taps_ref, lnw_ref, lnb_ref,
                 wos_ref, woc_ref, wo_ref, fnw_ref, o_ref, ub_s, uc_s, proj_s, act_s, coef_s):
    tm = x_ref.shape[0]
    nseq = tm // GRID_W
    wt_refs = (wt0_ref, wt1_ref, wt2_ref, wt3_ref, wt4_ref)

    def project(xv, mref):
        h = _modulated_norm(xv, nw_ref[...], mref).astype(BF16)
        for t in range(N_TAIL_PROJ):
            proj_s[t] = jnp.dot(h, wt_refs[t][...], preferred_element_type=F32)

    @pl.when(pl.program_id(0) == 0)
    def _():
        project(x_ref[...], mod_ref)
        coef_s[...] = jnp.dot(taps_ref[...], cw_ref[...], preferred_element_type=F32,
                              precision=lax.Precision.HIGHEST)

    ub_s[...] = (proj_s[0] * _sigmoid(proj_s[1])).astype(BF16)
    act_s[0] = _silu(proj_s[2])
    act_s[1] = _sigmoid(proj_s[3])
    act_s[2] = _sigmoid(proj_s[4])
    project(xnext_ref[...], modnext_ref)

    for q in range(nseq):
        rows = pl.ds(q * GRID_W, GRID_W)
        spec = jnp.dot(fwd_ref[...], ub_s[rows, :], preferred_element_type=F32)
        prod = _spectrum_product(spec, coef_s).astype(BF16)
        uc_s[rows, :] = jnp.dot(inv_ref[...], prod, preferred_element_type=F32) + cb_ref[...]
    x = x_ref[...]
    uc = uc_s[...]
    mu = jnp.mean(uc, axis=-1, keepdims=True)
    dev = uc - mu
    var = jnp.mean(dev * dev, axis=-1, keepdims=True)
    ln = dev * lax.rsqrt(var + EPS) * lnw_ref[...] + lnb_ref[...]
    u2 = _silu(ln) * act_s[0]
    branch_conf = jnp.dot(u2.astype(BF16), woc_ref[...], preferred_element_type=F32)
    branch_ssm = jnp.dot(yn_ref[...], wos_ref[...], preferred_element_type=F32)
    merged = act_s[1] * branch_ssm + act_s[2] * branch_conf
    out = jnp.dot(merged.astype(BF16), wo_ref[...], preferred_element_type=F32)
    xn = x + mod_ref[0, 2:3, :] * out
    ms = jnp.mean(xn * xn, axis=-1, keepdims=True)
    o_ref[...] = xn * lax.rsqrt(ms + EPS) * fnw_ref[...]


def _tail(yn, x2d, mod3, norm_w, w_all, conf_w, conf_b, ln_w, ln_b, w_os, w_oc, w_o, fn_w, *,
          tm, seq):
    m, d = x2d.shape
    tiles_per_batch = seq // tm
    const = lambda i: (0, 0)
    fwd, inv, taps = _conv_dft_constants()
    n_tiles = m // tm
    nxt = lambda i: jnp.minimum(i + 1, n_tiles - 1)
    once = dict(pipeline_mode=pl.Buffered(1))
    return pl.pallas_call(
        _tail_kernel,
        grid=(m // tm,),
        in_specs=[pl.BlockSpec((tm, D_INNER), lambda i: (i, 0)),
                  pl.BlockSpec((tm, d), lambda i: (i, 0)),
                  pl.BlockSpec((tm, d), lambda i: (nxt(i), 0)),
                  pl.BlockSpec((1, 3, d), lambda i: (i // tiles_per_batch, 0, 0)),
                  pl.BlockSpec((1, 3, d), lambda i: (nxt(i) // tiles_per_batch, 0, 0)),
                  pl.BlockSpec((1, d), const),
                  *[pl.BlockSpec((d, D_CONF), functools.partial(lambda t, i: (0, t), P_COLS // D_CONF + t),
                                 **once) for t in range(N_TAIL_PROJ)],
                  pl.BlockSpec((CONF_TAPS_PADDED, D_CONF), const),
                  pl.BlockSpec((1, D_CONF), const),
                  pl.BlockSpec(fwd.shape, const),
                  pl.BlockSpec(inv.shape, const),
                  pl.BlockSpec(taps.shape, const),
                  pl.BlockSpec((1, D_CONF), const),
                  pl.BlockSpec((1, D_CONF), const),
                  pl.BlockSpec((D_INNER, D_MODEL), const, **once),
                  pl.BlockSpec((D_CONF, D_MODEL), const, **once),
                  pl.BlockSpec((D_MODEL, D_MODEL), const, **once),
                  pl.BlockSpec((1, D_MODEL), const)],
        out_specs=pl.BlockSpec((tm, d), lambda i: (i, 0)),
        out_shape=jax.ShapeDtypeStruct((m, d), F32),
        scratch_shapes=[pltpu.VMEM((tm, D_CONF), BF16),
                        pltpu.VMEM((tm, D_CONF), F32),
                        pltpu.VMEM((N_TAIL_PROJ, tm, D_CONF), F32),
                        pltpu.VMEM((3, tm, D_CONF), F32),
                        pltpu.VMEM((3 * DFT_HALF, D_CONF), F32)],
        compiler_params=pltpu.CompilerParams(
            dimension_semantics=("arbitrary",), vmem_limit_bytes=VMEM_LIMIT_BYTES),
        name="tail",
    )(yn, x2d, x2d, mod3, mod3, norm_w, *([w_all] * N_TAIL_PROJ), conf_w, conf_b, fwd.astype(BF16),
      inv.astype(BF16), taps, ln_w, ln_b, w_os, w_oc, w_o, fn_w)


def _head_perm():
    return np.array([d * N_HEADS + g * HPG + r
                     for g in range(N_GROUPS) for d in range(2) for r in range(HPG)])


def kernel(x, c, ctx, c_ctx, w_mod, b_mod, norm_w, w_in, ssm_conv_w, ssm_conv_b, dt_bias, a_log,
           d_skip, ssm_norm_w, w_out_ssm, conf_conv_w, conf_conv_b, conf_ln_w, conf_ln_b,
           w_out_conf, w_out, final_norm_w):
    batch, seq, d = x.shape
    ctx_len = ctx.shape[1]
    assert w_in.shape[0] == 1, "single trunk layer"
    assert d == D_MODEL and seq % CHUNK == 0 and ctx_len % CHUNK == 0 and batch + 1 <= 8

    w_in0 = w_in[0]
    w_all = jnp.concatenate([w_in0[:, :C_END], w_in0[:, DT_END:]], axis=1).astype(BF16)
    perm = _head_perm()
    pad_heads = DT_LANES - 2 * N_HEADS
    w_dt = jnp.pad(w_in0[:, C_END:DT_END][:, perm], ((0, 0), (0, pad_heads))).astype(BF16)
    bias_p = jnp.pad(dt_bias[0].reshape(-1)[perm], (0, pad_heads)).reshape(1, DT_LANES)
    alog_p = jnp.pad(a_log[0].reshape(-1)[perm], (0, pad_heads)).reshape(1, DT_LANES)
    conv_w = ssm_conv_w[0]
    conv_b = ssm_conv_b[0].reshape(1, C_END)
    dskip_e = jnp.repeat(d_skip[0], HEAD_DIM).reshape(1, D_INNER)
    nw2 = norm_w[0].reshape(1, d)
    rest, expand, eye, scatter = _decay_constants()

    c_all = jnp.concatenate([c, c_ctx[None, :], jnp.zeros((8 - batch - 1, d), F32)], axis=0)
    mod3 = _mod(c_all, w_mod[0], b_mod[0]).reshape(8, 3, d)

    ctx_rows = batch * ctx_len
    tm_ctx = min(IN_PROJ_TM, ctx_rows)
    proj_ctx, dtraw_ctx = _in_proj(ctx.reshape(ctx_rows, d), mod3, nw2, w_all, w_dt,
                                   tm=tm_ctx, n_cols=B_END, plain_cols=B_END,
                                   mod_row=lambda i: batch)
    lhsp_ctx, rowpk_ctx = _dt_prep(dtraw_ctx, bias_p, alog_p, scatter, batch, ctx_len)
    h0 = _ctx_states(proj_ctx, lhsp_ctx, rowpk_ctx, conv_w, conv_b, expand, batch, ctx_len)

    m = batch * seq
    x2d = x.reshape(m, d)
    tm = min(IN_PROJ_TM, seq)
    tiles_per_batch = seq // tm
    proj, dtraw = _in_proj(x2d, mod3, nw2, w_all, w_dt, tm=tm, n_cols=P_COLS, plain_cols=C_END,
                           mod_row=lambda i: i // tiles_per_batch)
    lhsp, rowpk = _dt_prep(dtraw, bias_p, alog_p, scatter, batch, seq)
    yn = _ssd(proj, lhsp, rowpk, conv_w, conv_b, dskip_e, ssm_norm_w[0].reshape(1, D_INNER), h0,
              rest, expand, eye, batch, seq)
    conf_w = jnp.pad(conf_conv_w[0], ((0, CONF_TAPS_PADDED - CONF_KERNEL), (0, 0)))
    out = _tail(yn, x2d, mod3, nw2, w_all, conf_w, conf_conv_b[0].reshape(1, D_CONF),
                conf_ln_w[0].reshape(1, D_CONF), conf_ln_b[0].reshape(1, D_CONF),
                w_out_ssm[0].astype(BF16), w_out_conf[0].astype(BF16), w_out[0].astype(BF16),
                final_norm_w.reshape(1, d), tm=min(TAIL_TM, seq), seq=seq)
    return out.reshape(batch, seq, d)
```

```python
import functools

import numpy as np
import jax
import jax.numpy as jnp
from jax import lax
from jax.experimental import pallas as pl
from jax.experimental.pallas import tpu as pltpu

F32 = jnp.float32
BF16 = jnp.bfloat16

D_MODEL = 1024
GRID_W = 64
D_INNER = 2 * D_MODEL
HEAD_DIM = 64
N_HEADS = D_INNER // HEAD_DIM
N_GROUPS = 8
HPG = N_HEADS // N_GROUPS
D_STATE = 128
SSM_CONV = 4
CHUNK = 128
D_CONF = D_MODEL
CONF_KERNEL = 31
EPS = 1e-6

GN = N_GROUPS * D_STATE
X_END = D_INNER
B_END = X_END + GN
C_END = B_END + GN
DT_END = C_END + 2 * N_HEADS
Z_END = DT_END + D_INNER
GLU_END = Z_END + 2 * D_CONF
CG_END = GLU_END + D_CONF
IN_COLS = CG_END + 2 * D_MODEL

GROUP_W = HPG * HEAD_DIM
DT_LANES = 128
P_Z = C_END
P_COLS = C_END + D_INNER
IN_PROJ_TN = 1024
IN_PROJ_TM = 1024

HALO = 16
VMEM_LIMIT_BYTES = 58 * 1024 * 1024


def _sigmoid(v):
    return 1.0 / (1.0 + jnp.exp(-v))


def _silu(v):
    half = 0.5 * v
    return half + half * jnp.tanh(half)


def _mod_kernel(c_ref, w_ref, b_ref, o_ref):
    s = _silu(c_ref[...])
    o_ref[...] = jnp.dot(s, w_ref[...], preferred_element_type=F32,
                         precision=lax.Precision.HIGHEST) + b_ref[...]


def _mod(c_all, w_mod, b_mod):
    rows, d = c_all.shape
    n = w_mod.shape[1]
    tn = 1024
    return pl.pallas_call(
        _mod_kernel,
        grid=(n // tn,),
        in_specs=[pl.BlockSpec((rows, d), lambda j: (0, 0)),
                  pl.BlockSpec((d, tn), lambda j: (0, j)),
                  pl.BlockSpec((1, tn), lambda j: (0, j))],
        out_specs=pl.BlockSpec((rows, tn), lambda j: (0, j)),
        out_shape=jax.ShapeDtypeStruct((rows, n), F32),
        name="mod",
    )(c_all, w_mod, b_mod.reshape(1, n))


def _modulated_norm(x, nw, mod_ref):
    ms = jnp.mean(x * x, axis=-1, keepdims=True)
    y = x * lax.rsqrt(ms + EPS) * nw
    return y * (1.0 + mod_ref[0, 1:2, :]) + mod_ref[0, 0:1, :]


def _inproj_kernel(x_ref, mod_ref, nw_ref, w_ref, wdt_ref, o_ref, dt_ref, *, plain_cols, tn):
    hb = _modulated_norm(x_ref[...], nw_ref[...], mod_ref).astype(BF16)
    dt_ref[...] = jnp.dot(hb, wdt_ref[...], preferred_element_type=F32)
    for c0 in range(0, w_ref.shape[1], tn):
        acc = jnp.dot(hb, w_ref[:, c0:c0 + tn], preferred_element_type=F32)
        o_ref[:, c0:c0 + tn] = (acc if c0 < plain_cols else _silu(acc)).astype(BF16)


def _in_proj(x2d, mod3, norm_w, w_main, w_dt, *, tm, n_cols, plain_cols, mod_row):
    m, d = x2d.shape
    const = lambda i: (0, 0)
    once = dict(pipeline_mode=pl.Buffered(1))
    return pl.pallas_call(
        functools.partial(_inproj_kernel, plain_cols=plain_cols, tn=IN_PROJ_TN),
        grid=(m // tm,),
        in_specs=[pl.BlockSpec((tm, d), lambda i: (i, 0)),
                  pl.BlockSpec((1, 3, d), lambda i: (mod_row(i), 0, 0)),
                  pl.BlockSpec((1, d), const),
                  pl.BlockSpec((d, n_cols), const, **once),
                  pl.BlockSpec((d, DT_LANES), const, **once)],
        out_specs=[pl.BlockSpec((tm, n_cols), lambda i: (i, 0)),
                   pl.BlockSpec((tm, DT_LANES), lambda i: (i, 0))],
        out_shape=[jax.ShapeDtypeStruct((m, n_cols), BF16),
                   jax.ShapeDtypeStruct((m, DT_LANES), F32)],
        compiler_params=pltpu.CompilerParams(
            dimension_semantics=("parallel",), vmem_limit_bytes=VMEM_LIMIT_BYTES),
        name="in_proj",
    )(x2d, mod3, norm_w, w_main, w_dt)


N_PIECES = 3
HD = 2 * HPG
LHS_ONES = 0
LHS_LA = 32
ROW_WEND = N_PIECES * HD
ROWS_PER_GROUP = ROW_WEND + HD
MASK_BIG = 1e30
DT_FLOOR = 1e-37
LOG2E = 1.4426950408889634


def _decay_constants():
    k = np.arange(CHUNK)[:, None]
    s = np.arange(CHUNK)[None, :]
    rest = np.zeros((2 * CHUNK, HD * CHUNK), np.float32)
    expand = np.zeros((CHUNK, 2 * GROUP_W), np.float32)
    for h in range(HD):
        cols = slice(h * CHUNK, (h + 1) * CHUNK)
        hidden = (s > k) if h < HPG else (s < k)
        rest[CHUNK:, cols] = np.where(hidden, -MASK_BIG, 0.0)
        for j in range(N_PIECES):
            rest[LHS_LA + HD * j + h, cols] = 1.0
            expand[LHS_LA + HD * j + h, h * HEAD_DIM:(h + 1) * HEAD_DIM] = 1.0
    scatter = np.zeros((4 * DT_LANES, N_GROUPS * CHUNK), np.float32)
    for g in range(N_GROUPS):
        scatter[N_PIECES * DT_LANES, g * CHUNK + LHS_ONES:g * CHUNK + LHS_ONES + ROW_WEND] = 1.0
        for h in range(HD):
            for j in range(N_PIECES):
                scatter[j * DT_LANES + HD * g + h, g * CHUNK + LHS_LA + HD * j + h] = 1.0
    return (jnp.asarray(rest[ROWS_PER_GROUP:], BF16), jnp.asarray(expand, BF16),
            jnp.asarray(np.eye(CHUNK), BF16), jnp.asarray(scatter, BF16))


def _split3(v):
    p0 = v.astype(BF16)
    r1 = v - p0.astype(F32)
    p1 = r1.astype(BF16)
    p2 = (r1 - p1.astype(F32)).astype(BF16)
    return p0, p1, p2


def _dt_kernel(raw_ref, bias_ref, alog_ref, scatter_ref, lhsp_ref, rowpk_ref):
    n_chunks = raw_ref.shape[0] // CHUNK
    bias = bias_ref[...]
    a = -jnp.exp(alog_ref[...])
    row = lax.broadcasted_iota(jnp.int32, (CHUNK, CHUNK), 0)
    col = lax.broadcasted_iota(jnp.int32, (CHUNK, CHUNK), 1)
    tri_lo = (col <= row).astype(BF16)
    tri_up = (col >= row).astype(BF16)
    lane = lax.broadcasted_iota(jnp.int32, (1, DT_LANES), 1)
    is_fwd = (lane % HD) < HPG
    ones = jnp.ones((CHUNK, DT_LANES), BF16)

    def cumulate(tri, pieces):
        acc = jnp.dot(tri, pieces, preferred_element_type=F32)
        return acc[:, :DT_LANES] + acc[:, DT_LANES:2 * DT_LANES] + acc[:, 2 * DT_LANES:]

    for ci in range(n_chunks):
        rows = pl.ds(ci * CHUNK, CHUNK)
        v = raw_ref[rows, :] + bias
        dt = jnp.maximum(v, 0.0) + jnp.log1p(jnp.exp(-jnp.abs(v)))
        dta3 = jnp.concatenate(_split3(dt * a), axis=1)
        la = jnp.where(is_fwd, cumulate(tri_lo, dta3), cumulate(tri_up, dta3))
        la_end = jnp.where(is_fwd, la[CHUNK - 1:CHUNK, :], la[0:1, :])
        wend = jnp.exp(la_end - la) * dt
        la2 = la * LOG2E
        nla = jnp.log2(jnp.maximum(dt, DT_FLOOR)) - la2
        lhs = jnp.concatenate(list(_split3(la2)) + [ones], axis=1)
        lhsp_ref[rows, :] = jnp.dot(lhs, scatter_ref[...],
                                    preferred_element_type=F32).astype(lhsp_ref.dtype)
        pieces = [p.astype(F32) for p in _split3(nla.T)] + [wend.T]
        for g in range(N_GROUPS):
            for j, val in enumerate(pieces):
                rowpk_ref[0, pl.ds(ROWS_PER_GROUP * g + HD * j, HD), rows] = val[HD * g:HD * (g + 1), :]


DT_CHUNKS_PER_STEP = 8


def _dt_prep(dt_raw, bias_p, alog_p, scatter, batch, seq):
    nc = seq // CHUNK
    per = min(DT_CHUNKS_PER_STEP, nc)
    steps = nc // per
    rows = per * CHUNK
    const = lambda b, c: (0, 0)
    return pl.pallas_call(
        _dt_kernel,
        grid=(batch, steps),
        in_specs=[pl.BlockSpec((rows, DT_LANES), lambda b, c: (b * steps + c, 0)),
                  pl.BlockSpec((1, DT_LANES), const),
                  pl.BlockSpec((1, DT_LANES), const),
                  pl.BlockSpec(scatter.shape, const)],
        out_specs=[pl.BlockSpec((rows, N_GROUPS * CHUNK), lambda b, c: (b * steps + c, 0)),
                   pl.BlockSpec((1, N_GROUPS * ROWS_PER_GROUP, rows), lambda b, c: (b, 0, c))],
        out_shape=[jax.ShapeDtypeStruct((batch * seq, N_GROUPS * CHUNK), BF16),
                   jax.ShapeDtypeStruct((batch, N_GROUPS * ROWS_PER_GROUP, seq), F32)],
        compiler_params=pltpu.CompilerParams(dimension_semantics=("parallel", "parallel")),
        name="dt_prep",
    )(dt_raw, bias_p, alog_p, scatter)


def _chunk_start(c):
    return c * CHUNK if isinstance(c, int) else pl.multiple_of(c * CHUNK, CHUNK)


def _dft_conv_matrices(n, in_len, in_cols, out_len, lags, taps_padded):
    half = n // 2
    f = np.arange(half)
    ang = 2.0 * np.pi * np.outer(f, np.arange(in_cols)) / n
    fwd = np.zeros((n, in_cols))
    fwd[:half] = np.cos(ang)
    fwd[half] = np.cos(np.pi * np.arange(in_cols))
    fwd[half + 1:] = -np.sin(ang[1:])
    fwd[:, in_len:] = 0.0
    t = np.arange(out_len)
    angi = 2.0 * np.pi * np.outer(t, f) / n
    inv = np.zeros((out_len, n))
    inv[:, 0] = 1.0 / n
    inv[:, 1:half] = (2.0 / n) * np.cos(angi[:, 1:])
    inv[:, half] = np.cos(np.pi * t) / n
    inv[:, half + 1:] = -(2.0 / n) * np.sin(angi[:, 1:])
    n_taps = len(lags)
    angh = 2.0 * np.pi * np.outer(np.arange(half + 1), np.asarray(lags)) / n
    hre, him = np.cos(angh), -np.sin(angh)
    taps = np.zeros((3 * half, taps_padded))
    taps[:half, :n_taps] = hre[:half]
    taps[half:2 * half, :n_taps] = hre[:half]
    taps[half, :n_taps] = hre[half]
    taps[2 * half + 1:, :n_taps] = him[1:half]
    return jnp.asarray(fwd, F32), jnp.asarray(inv, F32), jnp.asarray(taps, F32)


def _spectrum_product(spec, coef_ref):
    half = spec.shape[0] // 2
    p_re, s_re, q_im = (coef_ref[pl.ds(i * half, half), :] for i in range(3))
    top, bot = spec[:half], spec[half:]
    return jnp.concatenate([top * p_re - bot * q_im, top * q_im + bot * s_re], axis=0)


def _conv_silu_chunk(refs, c, nc, w, b):
    seq = refs[0].shape[0]
    r0 = _chunk_start(c)
    if isinstance(c, int):
        p0, n0 = max(r0 - HALO, 0), min(r0 + CHUNK, seq - HALO)
    else:
        p0 = pl.multiple_of(jnp.maximum(r0 - HALO, 0), HALO)
        n0 = pl.multiple_of(jnp.minimum(r0 + CHUNK, seq - HALO), HALO)

    def rows(start, size):
        return jnp.concatenate([ref[pl.ds(start, size), :] for ref in refs], axis=1).astype(F32)

    prev, main, nxt = rows(p0, HALO), rows(r0, CHUNK), rows(n0, HALO)
    prev = jnp.where(c > 0, prev, 0.0)
    nxt = jnp.where(c < nc - 1, nxt, 0.0)
    ext = jnp.concatenate([prev, main, nxt], axis=0)
    total = CHUNK + 2 * HALO
    xm2 = pltpu.roll(ext, 2, 0)[HALO:HALO + CHUNK]
    xm1 = pltpu.roll(ext, 1, 0)[HALO:HALO + CHUNK]
    xp1 = pltpu.roll(ext, total - 1, 0)[HALO:HALO + CHUNK]
    y = w[0:1] * xm2 + w[1:2] * xm1 + w[2:3] * main + w[3:4] * xp1 + b
    return _silu(y)


def _chunk_operands(lhsp_ref, rowpk_ref, c):
    r0 = _chunk_start(c)
    return lhsp_ref[pl.ds(r0, CHUNK), :], rowpk_ref[0, :, pl.ds(r0, CHUNK)]


def _head_decay(lt, expand_ref, direction=None):
    if direction is None:
        ex = expand_ref[...]
    else:
        ex = expand_ref[:, direction * GROUP_W:(direction + 1) * GROUP_W]
    return jnp.exp2(jnp.dot(lt, ex, preferred_element_type=F32))


def _xs_blockdiag(xs_bf):
    lane = lax.broadcasted_iota(jnp.int32, xs_bf.shape, 1)
    zero = jnp.zeros_like(xs_bf)
    return jnp.concatenate(
        [jnp.where((lane >= r * HEAD_DIM) & (lane < (r + 1) * HEAD_DIM), xs_bf, zero)
         for r in range(HPG)], axis=0)


def _bt_weighted(bt_bf, rp, direction):
    base = ROW_WEND + direction * HPG
    w = rp[base:base + HPG, :].astype(BF16)
    return jnp.concatenate([bt_bf * w[r:r + 1, :] for r in range(HPG)], axis=1)


def _ctx_kernel(xs_ref, b_ref, lhsp_ref, rowpk_ref, cwx_ref, cbx_ref, cwb_ref, cbb_ref,
                expand_ref, o_ref):
    nc = xs_ref.shape[0] // CHUNK
    cw = jnp.concatenate([cwx_ref[...], cwb_ref[...]], axis=1)
    cb = jnp.concatenate([cbx_ref[...], cbb_ref[...]], axis=1)
    contrib, decay = [], []
    for c in range(nc):
        xb = _conv_silu_chunk((xs_ref, b_ref), c, nc, cw, cb)
        xs_bf = xb[:, :GROUP_W].astype(BF16)
        bt = xb[:, GROUP_W:].T.astype(BF16)
        lt, rp = _chunk_operands(lhsp_ref, rowpk_ref, c)
        lhs = jnp.concatenate([_bt_weighted(bt, rp, 0), _bt_weighted(bt, rp, 1)], axis=0)
        contrib.append(jnp.dot(lhs, _xs_blockdiag(xs_bf), preferred_element_type=F32))
        decay.append(_head_decay(lt, expand_ref))
    hf = jnp.zeros((D_STATE, GROUP_W), F32)
    for c in range(nc):
        hf = hf * decay[c][CHUNK - 1:CHUNK, :GROUP_W] + contrib[c][:D_STATE]
    hb = jnp.zeros((D_STATE, GROUP_W), F32)
    for c in reversed(range(nc)):
        hb = hb * decay[c][0:1, GROUP_W:] + contrib[c][D_STATE:]
    o_ref[0, 0, 0] = hf
    o_ref[0, 0, 1] = hb


def _ctx_states(proj_ctx, lhsp, rowpk, conv_w, conv_b, expand, batch, seq):
    bcol = X_END // D_STATE
    const = lambda b, g: (0, 0)
    return pl.pallas_call(
        _ctx_kernel,
        grid=(batch, N_GROUPS),
        in_specs=[pl.BlockSpec((seq, GROUP_W), lambda b, g: (b, g)),
                  pl.BlockSpec((seq, D_STATE), lambda b, g: (b, bcol + g)),
                  pl.BlockSpec((seq, CHUNK), lambda b, g: (b, g)),
                  pl.BlockSpec((1, ROWS_PER_GROUP, seq), lambda b, g: (b, g, 0)),
                  pl.BlockSpec((SSM_CONV, GROUP_W), lambda b, g: (0, g)),
                  pl.BlockSpec((1, GROUP_W), lambda b, g: (0, g)),
                  pl.BlockSpec((SSM_CONV, D_STATE), lambda b, g: (0, bcol + g)),
                  pl.BlockSpec((1, D_STATE), lambda b, g: (0, bcol + g)),
                  pl.BlockSpec(expand.shape, const)],
        out_specs=pl.BlockSpec((1, 1, 2, D_STATE, GROUP_W), lambda b, g: (b, g, 0, 0, 0)),
        out_shape=jax.ShapeDtypeStruct((batch, N_GROUPS, 2, D_STATE, GROUP_W), F32),
        compiler_params=pltpu.CompilerParams(dimension_semantics=("parallel", "parallel")),
        name="ctx_states",
    )(proj_ctx, proj_ctx, lhsp, rowpk, conv_w, conv_b, conv_w, conv_b, expand)


def _ssd_kernel(xs_ref, b_ref, c_ref, z_ref, lhsp_ref, rowpk_ref, cwx_ref, cbx_ref, cwb_ref,
                cbb_ref, cwc_ref, cbc_ref, dsk_ref, nw_ref, h0_ref, rest_ref, expand_ref, eye_ref,
                o_ref, cx_s, cbt_s, cc_s, stash_s, hf_s, hb_s):
    seq = xs_ref.shape[0]
    nc = seq // CHUNK
    cw = jnp.concatenate([cwx_ref[...], cwb_ref[...], cwc_ref[...]], axis=1)
    cb =jnp.concatenate([cbx_ref[...], cbb_ref[...], cbc_ref[...]], axis=1)
    hf_s[...] = h0_ref[0, 0, 0]
    hb_s[...] = h0_ref[0, 0, 1]

    def back_body(i, carry):
        c = nc - 1 - i
        r0 = _chunk_start(c)
        xbc = _conv_silu_chunk((xs_ref, b_ref, c_ref), c, nc, cw, cb)
        xs_bf = xbc[:, :GROUP_W].astype(BF16)
        bt = xbc[:, GROUP_W:GROUP_W + D_STATE].T.astype(BF16)
        cx_s[pl.ds(r0, CHUNK), :] = xs_bf
        cbt_s[pl.ds(r0, CHUNK), :] = bt
        cc_s[pl.ds(r0, CHUNK), :] = xbc[:, GROUP_W + D_STATE:].astype(BF16)
        lt, rp = _chunk_operands(lhsp_ref, rowpk_ref, c)
        hb = hb_s[...]
        stash_s[pl.ds(r0, CHUNK), :] = hb.astype(BF16)
        dec = _head_decay(lt, expand_ref, 1)[0:1, :]
        contrib = jnp.dot(_bt_weighted(bt, rp, 1), _xs_blockdiag(xs_bf), preferred_element_type=F32)
        hb_s[...] = hb * dec + contrib
        return carry

    lax.fori_loop(0, nc, back_body, 0, unroll=16)

    prow = lax.broadcasted_iota(jnp.int32, (ROWS_PER_GROUP, HD * CHUNK), 0)
    pcol = lax.broadcasted_iota(jnp.int32, (ROWS_PER_GROUP, HD * CHUNK), 1)
    own_block = (prow < ROW_WEND) & (prow % HD == pcol // CHUNK)
    dsk, nw = dsk_ref[...], nw_ref[...]

    def fwd_body(c, carry):
        r0 = _chunk_start(c)
        xs_bf = cx_s[pl.ds(r0, CHUNK), :]
        bt = cbt_s[pl.ds(r0, CHUNK), :]
        cc = cc_s[pl.ds(r0, CHUNK), :]
        lt, rp = _chunk_operands(lhsp_ref, rowpk_ref, c)
        scores = jnp.dot(cc, bt, preferred_element_type=F32)
        rows = jnp.concatenate([rp.astype(BF16)] * HD, axis=1)
        rows = jnp.where(own_block, rows, jnp.zeros_like(rows))
        seg = jnp.dot(jnp.concatenate([lt, eye_ref[...]], axis=1),
                      jnp.concatenate([rows, rest_ref[...]], axis=0), preferred_element_type=F32)
        decay_dt = jnp.exp2(seg)
        mix = jnp.concatenate(
            [(scores * (decay_dt[:, r * CHUNK:(r + 1) * CHUNK]
                        + decay_dt[:, (HPG + r) * CHUNK:(HPG + r + 1) * CHUNK])).astype(BF16)
             for r in range(HPG)], axis=1)
        lhs = jnp.concatenate([mix, _bt_weighted(bt, rp, 0)], axis=0)
        prod = jnp.dot(lhs, _xs_blockdiag(xs_bf), preferred_element_type=F32)
        hf = hf_s[...]
        e = _head_decay(lt, expand_ref)
        states = jnp.concatenate([hf.astype(BF16), stash_s[pl.ds(r0, CHUNK), :]], axis=1)
        carried = jnp.dot(cc, states, preferred_element_type=F32) * e
        y = dsk * xs_bf.astype(F32) + prod[:CHUNK] + carried[:, :GROUP_W] + carried[:, GROUP_W:]
        hf_s[...] = hf * e[CHUNK - 1:CHUNK, :GROUP_W] + prod[CHUNK:]
        y = y * z_ref[pl.ds(r0, CHUNK), :].astype(F32)
        ms = jnp.mean(y * y, axis=-1, keepdims=True)
        o_ref[pl.ds(r0, CHUNK), :] = (y * lax.rsqrt(ms + EPS) * nw).astype(o_ref.dtype)
        return carry

    lax.fori_loop(0, nc, fwd_body, 0, unroll=32)


def _ssd(proj, lhsp, rowpk, conv_w, conv_b, dskip_e, norm_w, h0, rest, expand, eye, batch, seq):
    bcol = X_END // D_STATE
    ccol = B_END // D_STATE
    zcol = P_Z // GROUP_W
    gmap = lambda b, g: (0, g)
    const = lambda b, g: (0, 0)
    return pl.pallas_call(
        _ssd_kernel,
        grid=(batch, N_GROUPS),
        in_specs=[pl.BlockSpec((seq, GROUP_W), lambda b, g: (b, g)),
                  pl.BlockSpec((seq, D_STATE), lambda b, g: (b, bcol + g)),
                  pl.BlockSpec((seq, D_STATE), lambda b, g: (b, ccol + g)),
                  pl.BlockSpec((seq, GROUP_W), lambda b, g: (b, zcol + g)),
                  pl.BlockSpec((seq, CHUNK), lambda b, g: (b, g)),
                  pl.BlockSpec((1, ROWS_PER_GROUP, seq), lambda b, g: (b, g, 0)),
                  pl.BlockSpec((SSM_CONV, GROUP_W), gmap),
                  pl.BlockSpec((1, GROUP_W), gmap),
                  pl.BlockSpec((SSM_CONV, D_STATE), lambda b, g: (0, bcol + g)),
                  pl.BlockSpec((1, D_STATE), lambda b, g: (0, bcol + g)),
                  pl.BlockSpec((SSM_CONV, D_STATE), lambda b, g: (0, ccol + g)),
                  pl.BlockSpec((1, D_STATE), lambda b, g: (0, ccol + g)),
                  pl.BlockSpec((1, GROUP_W), gmap),
                  pl.BlockSpec((1, GROUP_W), gmap),
                  pl.BlockSpec((1, 1, 2, D_STATE, GROUP_W), lambda b, g: (b, g, 0, 0, 0)),
                  pl.BlockSpec(rest.shape, const),
                  pl.BlockSpec(expand.shape, const),
                  pl.BlockSpec(eye.shape, const)],
        out_specs=pl.BlockSpec((seq, GROUP_W), lambda b, g: (b, g)),
        out_shape=jax.ShapeDtypeStruct((batch * seq, D_INNER), BF16),
        scratch_shapes=[pltpu.VMEM((seq, GROUP_W), BF16),
                        pltpu.VMEM((seq, D_STATE), BF16),
                        pltpu.VMEM((seq, D_STATE), BF16),
                        pltpu.VMEM((seq, GROUP_W), BF16),
                        pltpu.VMEM((D_STATE, GROUP_W), F32),
                        pltpu.VMEM((D_STATE, GROUP_W), F32)],
        compiler_params=pltpu.CompilerParams(
            dimension_semantics=("parallel", "parallel"), vmem_limit_bytes=VMEM_LIMIT_BYTES),
        name="ssd",
    )(proj, proj, proj, proj, lhsp, rowpk, conv_w, conv_b, conv_w, conv_b, conv_w, conv_b,
      dskip_e, norm_w, h0, rest, expand, eye)


N_TAIL_PROJ = 5
TAIL_TM = 512


DFT_N = 2 * GRID_W
DFT_HALF = DFT_N // 2
CONF_TAPS_PADDED = 32


def _conv_dft_constants():
    lags = [CONF_KERNEL // 2 - k for k in range(CONF_KERNEL)]
    return _dft_conv_matrices(DFT_N, GRID_W, GRID_W, GRID_W, lags, CONF_TAPS_PADDED)


def _tail_kernel(yn_ref, x_ref, xnext_ref, mod_ref, modnext_ref, nw_ref, wt0_ref, wt1_ref, wt2_ref,
                 wt3_ref, wt4_ref, cw_ref, cb_ref, fwd_ref, inv_ref, taps_ref, lnw_ref, lnb_ref,
                 wos_ref, woc_ref, wo_ref, fnw_ref, o_ref, ub_s, uc_s, proj_s, act_s, coef_s):
    tm = x_ref.shape[0]
    nseq = tm // GRID_W
    wt_refs = (wt0_ref, wt1_ref, wt2_ref, wt3_ref, wt4_ref)

    def project(xv, mref):
        h = _modulated_norm(xv, nw_ref[...], mref).astype(BF16)
        for t in range(N_TAIL_PROJ):
            proj_s[t] = jnp.dot(h, wt_refs[t][...], preferred_element_type=F32)

    @pl.when(pl.program_id(0) == 0)
    def _():
        project(x_ref[...], mod_ref)
        coef_s[...] = jnp.dot(taps_ref[...], cw_ref[...], preferred_element_type=F32,
                              precision=lax.Precision.HIGHEST)

    ub_s[...] = (proj_s[0] * _sigmoid(proj_s[1])).astype(BF16)
    act_s[0] = _silu(proj_s[2])
    act_s[1] = _sigmoid(proj_s[3])
    act_s[2] = _sigmoid(proj_s[4])
    project(xnext_ref[...], modnext_ref)

    for q in range(nseq):
        rows = pl.ds(q * GRID_W, GRID_W)
        spec = jnp.dot(fwd_ref[...], ub_s[rows, :], preferred_element_type=F32)
        prod = _spectrum_product(spec, coef_s).astype(BF16)
        uc_s[rows, :] = jnp.dot(inv_ref[...], prod, preferred_element_type=F32) + cb_ref[...]
    x = x_ref[...]
    uc = uc_s[...]
    mu = jnp.mean(uc, axis=-1, keepdims=True)
    dev = uc - mu
    var = jnp.mean(dev * dev, axis=-1, keepdims=True)
    ln = dev * lax.rsqrt(var + EPS) * lnw_ref[...] + lnb_ref[...]
    u2 = _silu(ln) * act_s[0]
    branch_conf = jnp.dot(u2.astype(BF16), woc_ref[...], preferred_element_type=F32)
    branch_ssm = jnp.dot(yn_ref[...], wos_ref[...], preferred_element_type=F32)
    merged = act_s[1] * branch_ssm + act_s[2] * branch_conf
    out = jnp.dot(merged.astype(BF16), wo_ref[...], preferred_element_type=F32)
    xn = x + mod_ref[0, 2:3, :] * out
    ms = jnp.mean(xn * xn, axis=-1, keepdims=True)
    o_ref[...] = xn * lax.rsqrt(ms + EPS) * fnw_ref[...]


def _tail(yn, x2d, mod3, norm_w, w_all, conf_w, conf_b, ln_w, ln_b, w_os, w_oc, w_o, fn_w, *,
          tm, seq):
    m, d = x2d.shape
    tiles_per_batch = seq // tm
    const = lambda i: (0, 0)
    fwd, inv, taps = _conv_dft_constants()
    n_tiles = m // tm
    nxt = lambda i: jnp.minimum(i + 1, n_tiles - 1)
    once = dict(pipeline_mode=pl.Buffered(1))
    return pl.pallas_call(
        _tail_kernel,
        grid=(m // tm,),
        in_specs=[pl.BlockSpec((tm, D_INNER), lambda i: (i, 0)),
                  pl.BlockSpec((tm, d), lambda i: (i, 0)),
                  pl.BlockSpec((tm, d), lambda i: (nxt(i), 0)),
                  pl.BlockSpec((1, 3, d), lambda i: (i // tiles_per_batch, 0, 0)),
                  pl.BlockSpec((1, 3, d), lambda i: (nxt(i) // tiles_per_batch, 0, 0)),
                  pl.BlockSpec((1, d), const),
                  *[pl.BlockSpec((d, D_CONF), functools.partial(lambda t, i: (0, t), P_COLS // D_CONF + t),
                                 **once) for t in range(N_TAIL_PROJ)],
                  pl.BlockSpec((CONF_TAPS_PADDED, D_CONF), const),
                  pl.BlockSpec((1, D_CONF), const),
                  pl.BlockSpec(fwd.shape, const),
                  pl.BlockSpec(inv.shape, const),
                  pl.BlockSpec(taps.shape, const),
                  pl.BlockSpec((1, D_CONF), const),
                  pl.BlockSpec((1, D_CONF), const),
                  pl.BlockSpec((D_INNER, D_MODEL), const, **once),
                  pl.BlockSpec((D_CONF, D_MODEL), const, **once),
                  pl.BlockSpec((D_MODEL, D_MODEL), const, **once),
                  pl.BlockSpec((1, D_MODEL), const)],
        out_specs=pl.BlockSpec((tm, d), lambda i: (i, 0)),
        out_shape=jax.ShapeDtypeStruct((m, d), F32),
        scratch_shapes=[pltpu.VMEM((tm, D_CONF), BF16),
                        pltpu.VMEM((tm, D_CONF), F32),
                        pltpu.VMEM((N_TAIL_PROJ, tm, D_CONF), F32),
                        pltpu.VMEM((3, tm, D_CONF), F32),
                        pltpu.VMEM((3 * DFT_HALF, D_CONF), F32)],
        compiler_params=pltpu.CompilerParams(
            dimension_semantics=("arbitrary",), vmem_limit_bytes=VMEM_LIMIT_BYTES),
        name="tail",
    )(yn, x2d, x2d, mod3, mod3, norm_w, *([w_all] * N_TAIL_PROJ), conf_w, conf_b, fwd.astype(BF16),
      inv.astype(BF16), taps, ln_w, ln_b, w_os, w_oc, w_o, fn_w)


def _head_perm():
    return np.array([d * N_HEADS + g * HPG + r
                     for g in range(N_GROUPS) for d in range(2) for r in range(HPG)])


def kernel(x, c, ctx, c_ctx, w_mod, b_mod, norm_w, w_in, ssm_conv_w, ssm_conv_b, dt_bias, a_log,
           d_skip, ssm_norm_w, w_out_ssm, conf_conv_w, conf_conv_b, conf_ln_w, conf_ln_b,
           w_out_conf, w_out, final_norm_w):
    batch, seq, d = x.shape
    ctx_len = ctx.shape[1]
    assert w_in.shape[0] == 1, "single trunk layer"
    assert d == D_MODEL and seq % CHUNK == 0 and ctx_len % CHUNK == 0 and batch + 1 <= 8

    w_in0 = w_in[0]
    w_all = jnp.concatenate([w_in0[:, :C_END], w_in0[:, DT_END:]], axis=1).astype(BF16)
    perm = _head_perm()
    pad_heads = DT_LANES - 2 * N_HEADS
    w_dt = jnp.pad(w_in0[:, C_END:DT_END][:, perm], ((0, 0), (0, pad_heads))).astype(BF16)
    bias_p = jnp.pad(dt_bias[0].reshape(-1)[perm], (0, pad_heads)).reshape(1, DT_LANES)
    alog_p = jnp.pad(a_log[0].reshape(-1)[perm], (0, pad_heads)).reshape(1, DT_LANES)
    conv_w = ssm_conv_w[0]
    conv_b = ssm_conv_b[0].reshape(1, C_END)
    dskip_e = jnp.repeat(d_skip[0], HEAD_DIM).reshape(1, D_INNER)
    nw2 = norm_w[0].reshape(1, d)
    rest, expand, eye, scatter = _decay_constants()

    c_all = jnp.concatenate([c, c_ctx[None, :], jnp.zeros((8 - batch - 1, d), F32)], axis=0)
    mod3 = _mod(c_all, w_mod[0], b_mod[0]).reshape(8, 3, d)

    ctx_rows = batch * ctx_len
    tm_ctx = min(IN_PROJ_TM, ctx_rows)
    proj_ctx, dtraw_ctx = _in_proj(ctx.reshape(ctx_rows, d), mod3, nw2, w_all, w_dt,
                                   tm=tm_ctx, n_cols=B_END, plain_cols=B_END,
                                   mod_row=lambda i: batch)
    lhsp_ctx, rowpk_ctx = _dt_prep(dtraw_ctx, bias_p, alog_p, scatter, batch, ctx_len)
    h0 = _ctx_states(proj_ctx, lhsp_ctx, rowpk_ctx, conv_w, conv_b, expand, batch, ctx_len)

    m = batch * seq
    x2d = x.reshape(m, d)
    tm = min(IN_PROJ_TM, seq)
    tiles_per_batch = seq // tm
    proj, dtraw = _in_proj(x2d, mod3, nw2, w_all, w_dt, tm=tm, n_cols=P_COLS, plain_cols=C_END,
                           mod_row=lambda i: i // tiles_per_batch)
    lhsp, rowpk = _dt_prep(dtraw, bias_p, alog_p, scatter, batch, seq)
    yn = _ssd(proj, lhsp, rowpk, conv_w, conv_b, dskip_e, ssm_norm_w[0].reshape(1, D_INNER), h0,
              rest, expand, eye, batch, seq)
    conf_w = jnp.pad(conf_conv_w[0], ((0, CONF_TAPS_PADDED - CONF_KERNEL), (0, 0)))
    out = _tail(yn, x2d, mod3, nw2, w_all, conf_w, conf_conv_b[0].reshape(1, D_CONF),
                conf_ln_w[0].reshape(1, D_CONF), conf_ln_b[0].reshape(1, D_CONF),
                w_out_ssm[0].astype(BF16), w_out_conf[0].astype(BF16), w_out[0].astype(BF16),
                final_norm_w.reshape(1, d), tm=min(TAIL_TM, seq), seq=seq)
    return out.reshape(batch, seq, d)
```

```python
import functools

import numpy as np
import jax
import jax.numpy as jnp
from jax import lax
from jax.experimental import pallas as pl
from jax.experimental.pallas import tpu as pltpu

F32 = jnp.float32
BF16 = jnp.bfloat16

D_MODEL = 1024
GRID_W = 64
D_INNER = 2 * D_MODEL
HEAD_DIM = 64
N_HEADS = D_INNER // HEAD_DIM
N_GROUPS = 8
HPG = N_HEADS // N_GROUPS
D_STATE = 128
SSM_CONV = 4
CHUNK = 128
D_CONF = D_MODEL
CONF_KERNEL = 31
EPS = 1e-6

GN = N_GROUPS * D_STATE
X_END = D_INNER
B_END = X_END + GN
C_END = B_END + GN
DT_END = C_END + 2 * N_HEADS
Z_END = DT_END + D_INNER
GLU_END = Z_END + 2 * D_CONF
CG_END = GLU_END + D_CONF
IN_COLS = CG_END + 2 * D_MODEL

GROUP_W = HPG * HEAD_DIM
DT_LANES = 128
P_Z = C_END
P_COLS = C_END + D_INNER
IN_PROJ_TN = 1024
SLAB_B = GROUP_W
SLAB_C = SLAB_B + D_STATE
SLAB_Z = SLAB_C + D_STATE
SLAB_W = SLAB_Z + GROUP_W
IN_PROJ_TM = 1024

HALO = 16
VMEM_LIMIT_BYTES = 58 * 1024 * 1024


def _sigmoid(v):
    return 1.0 / (1.0 + jnp.exp(-v))


def _silu(v):
    half = 0.5 * v
    return half + half * jnp.tanh(half)


def _mod_kernel(c_ref, w_ref, b_ref, o_ref):
    s = _silu(c_ref[...])
    o_ref[...] = jnp.dot(s, w_ref[...], preferred_element_type=F32,
                         precision=lax.Precision.HIGHEST) + b_ref[...]


def _mod(c_all, w_mod, b_mod):
    rows, d = c_all.shape
    n = w_mod.shape[1]
    tn = 1024
    return pl.pallas_call(
        _mod_kernel,
        grid=(n // tn,),
        in_specs=[pl.BlockSpec((rows, d), lambda j: (0, 0)),
                  pl.BlockSpec((d, tn), lambda j: (0, j)),
                  pl.BlockSpec((1, tn), lambda j: (0, j))],
        out_specs=pl.BlockSpec((rows, tn), lambda j: (0, j)),
        out_shape=jax.ShapeDtypeStruct((rows, n), F32),
        name="mod",
    )(c_all, w_mod, b_mod.reshape(1, n))


def _modulated_norm(x, nw, mod_ref):
    ms = jnp.mean(x * x, axis=-1, keepdims=True)
    y = x * lax.rsqrt(ms + EPS) * nw
    return y * (1.0 + mod_ref[0, 1:2, :]) + mod_ref[0, 0:1, :]


def _inproj_kernel(x_ref, mod_ref, nw_ref, w_ref, wdt_ref, o_ref, dt_ref, *, plain_cols, tn):
    del plain_cols, tn
    hb = _modulated_norm(x_ref[...], nw_ref[...], mod_ref).astype(BF16)
    dt_ref[...] = jnp.dot(hb, wdt_ref[...], preferred_element_type=F32)
    for c0 in range(0, w_ref.shape[1], SLAB_W):
        acc = jnp.dot(hb, w_ref[:, c0:c0 + SLAB_W], preferred_element_type=F32)
        o_ref[:, c0:c0 + SLAB_Z] = acc[:, :SLAB_Z].astype(BF16)
        o_ref[:, c0 + SLAB_Z:c0 + SLAB_W] = _silu(acc[:, SLAB_Z:]).astype(BF16)


def _in_proj(x2d, mod3, norm_w, w_main, w_dt, *, tm, n_cols, plain_cols, mod_row):
    m, d = x2d.shape
    const = lambda i: (0, 0)
    once = dict(pipeline_mode=pl.Buffered(1))
    return pl.pallas_call(
        functools.partial(_inproj_kernel, plain_cols=plain_cols, tn=IN_PROJ_TN),
        grid=(m // tm,),
        in_specs=[pl.BlockSpec((tm, d), lambda i: (i, 0)),
                  pl.BlockSpec((1, 3, d), lambda i: (mod_row(i), 0, 0)),
                  pl.BlockSpec((1, d), const),
                  pl.BlockSpec((d, n_cols), const, **once),
                  pl.BlockSpec((d, DT_LANES), const, **once)],
        out_specs=[pl.BlockSpec((tm, n_cols), lambda i: (i, 0)),
                   pl.BlockSpec((tm, DT_LANES), lambda i: (i, 0))],
        out_shape=[jax.ShapeDtypeStruct((m, n_cols), BF16),
                   jax.ShapeDtypeStruct((m, DT_LANES), F32)],
        compiler_params=pltpu.CompilerParams(
            dimension_semantics=("parallel",), vmem_limit_bytes=VMEM_LIMIT_BYTES),
        name="in_proj",
    )(x2d, mod3, norm_w, w_main, w_dt)


N_PIECES = 3
HD = 2 * HPG
LHS_ONES = 0
LHS_LA = 32
ROW_WEND = N_PIECES * HD
ROWS_PER_GROUP = ROW_WEND + HD
MASK_BIG = 1e30
DT_FLOOR = 1e-37
LOG2E = 1.4426950408889634


def _decay_constants():
    k = np.arange(CHUNK)[:, None]
    s = np.arange(CHUNK)[None, :]
    rest = np.zeros((2 * CHUNK, HD * CHUNK), np.float32)
    expand = np.zeros((CHUNK, 2 * GROUP_W), np.float32)
    for h in range(HD):
        cols = slice(h * CHUNK, (h + 1) * CHUNK)
        hidden = (s > k) if h < HPG else (s < k)
        rest[CHUNK:, cols] = np.where(hidden, -MASK_BIG, 0.0)
        for j in range(N_PIECES):
            rest[LHS_LA + HD * j + h, cols] = 1.0
            expand[LHS_LA + HD * j + h, h * HEAD_DIM:(h + 1) * HEAD_DIM] = 1.0
    scatter = np.zeros((4 * DT_LANES, N_GROUPS * CHUNK), np.float32)
    for g in range(N_GROUPS):
        scatter[N_PIECES * DT_LANES, g * CHUNK + LHS_ONES:g * CHUNK + LHS_ONES + ROW_WEND] = 1.0
        for h in range(HD):
            for j in range(N_PIECES):
                scatter[j * DT_LANES + HD * g + h, g * CHUNK + LHS_LA + HD * j + h] = 1.0
    return (jnp.asarray(rest[ROWS_PER_GROUP:], BF16), jnp.asarray(expand, BF16),
            jnp.asarray(np.eye(CHUNK), BF16), jnp.asarray(scatter, BF16))


def _split3(v):
    p0 = v.astype(BF16)
    r1 = v - p0.astype(F32)
    p1 = r1.astype(BF16)
    p2 = (r1 - p1.astype(F32)).astype(BF16)
    return p0, p1, p2


def _dt_kernel(raw_ref, bias_ref, alog_ref, scatter_ref, lhsp_ref, rowpk_ref):
    n_chunks = raw_ref.shape[0] // CHUNK
    bias = bias_ref[...]
    a = -jnp.exp(alog_ref[...])
    row = lax.broadcasted_iota(jnp.int32, (CHUNK, CHUNK), 0)
    col = lax.broadcasted_iota(jnp.int32, (CHUNK, CHUNK), 1)
    tri_lo = (col <= row).astype(BF16)
    tri_up = (col >= row).astype(BF16)
    lane = lax.broadcasted_iota(jnp.int32, (1, DT_LANES), 1)
    is_fwd = (lane % HD) < HPG
    ones = jnp.ones((CHUNK, DT_LANES), BF16)

    def cumulate(tri, pieces):
        acc = jnp.dot(tri, pieces, preferred_element_type=F32)
        return acc[:, :DT_LANES] + acc[:, DT_LANES:2 * DT_LANES] + acc[:, 2 * DT_LANES:]

    for ci in range(n_chunks):
        rows = pl.ds(ci * CHUNK, CHUNK)
        v = raw_ref[rows, :] + bias
        dt = jnp.maximum(v, 0.0) + jnp.log1p(jnp.exp(-jnp.abs(v)))
        dta3 = jnp.concatenate(_split3(dt * a), axis=1)
        la = jnp.where(is_fwd, cumulate(tri_lo, dta3), cumulate(tri_up, dta3))
        la_end = jnp.where(is_fwd, la[CHUNK - 1:CHUNK, :], la[0:1, :])
        wend = jnp.exp(la_end - la) * dt
        la2 = la * LOG2E
        nla = jnp.log2(jnp.maximum(dt, DT_FLOOR)) - la2
        lhs = jnp.concatenate(list(_split3(la2)) + [ones], axis=1)
        lhsp_ref[rows, :] = jnp.dot(lhs, scatter_ref[...],
                                    preferred_element_type=F32).astype(lhsp_ref.dtype)
        pieces = [p.astype(F32) for p in _split3(nla.T)] + [wend.T]
        for g in range(N_GROUPS):
            for j, val in enumerate(pieces):
                rowpk_ref[0, pl.ds(ROWS_PER_GROUP * g + HD * j, HD), rows] = val[HD * g:HD * (g + 1), :]


DT_CHUNKS_PER_STEP = 8


def _dt_prep(dt_raw, bias_p, alog_p, scatter, batch, seq):
    nc = seq // CHUNK
    per = min(DT_CHUNKS_PER_STEP, nc)
    steps = nc // per
    rows = per * CHUNK
    const = lambda b, c: (0, 0)
    return pl.pallas_call(
        _dt_kernel,
        grid=(batch, steps),
        in_specs=[pl.BlockSpec((rows, DT_LANES), lambda b, c: (b * steps + c, 0)),
                  pl.BlockSpec((1, DT_LANES), const),
                  pl.BlockSpec((1, DT_LANES), const),
                  pl.BlockSpec(scatter.shape, const)],
        out_specs=[pl.BlockSpec((rows, N_GROUPS * CHUNK), lambda b, c: (b * steps + c, 0)),
                   pl.BlockSpec((1, N_GROUPS * ROWS_PER_GROUP, rows), lambda b, c: (b, 0, c))],
        out_shape=[jax.ShapeDtypeStruct((batch * seq, N_GROUPS * CHUNK), BF16),
                   jax.ShapeDtypeStruct((batch, N_GROUPS * ROWS_PER_GROUP, seq), F32)],
        compiler_params=pltpu.CompilerParams(dimension_semantics=("parallel", "parallel")),
        name="dt_prep",
    )(dt_raw, bias_p, alog_p, scatter)


def _chunk_start(c):
    return c * CHUNK if isinstance(c, int) else pl.multiple_of(c * CHUNK, CHUNK)


def _dft_conv_matrices(n, in_len, in_cols, out_len, lags, taps_padded):
    half = n // 2
    f = np.arange(half)
    ang = 2.0 * np.pi * np.outer(f, np.arange(in_cols)) / n
    fwd = np.zeros((n, in_cols))
    fwd[:half] = np.cos(ang)
    fwd[half] = np.cos(np.pi * np.arange(in_cols))
    fwd[half + 1:] = -np.sin(ang[1:])
    fwd[:, in_len:] = 0.0
    t = np.arange(out_len)
    angi = 2.0 * np.pi * np.outer(t, f) / n
    inv = np.zeros((out_len, n))
    inv[:, 0] = 1.0 / n
    inv[:, 1:half] = (2.0 / n) * np.cos(angi[:, 1:])
    inv[:, half] = np.cos(np.pi * t) / n
    inv[:, half + 1:] = -(2.0 / n) * np.sin(angi[:, 1:])
    n_taps = len(lags)
    angh = 2.0 * np.pi * np.outer(np.arange(half + 1), np.asarray(lags)) / n
    hre, him = np.cos(angh), -np.sin(angh)
    taps = np.zeros((3 * half, taps_padded))
    taps[:half, :n_taps] = hre[:half]
    taps[half:2 * half, :n_taps] = hre[:half]
    taps[half, :n_taps] = hre[half]
    taps[2 * half + 1:, :n_taps] = him[1:half]
    return jnp.asarray(fwd, F32), jnp.asarray(inv, F32), jnp.asarray(taps, F32)


def _spectrum_product(spec, coef_ref):
    half = spec.shape[0] // 2
    p_re, s_re, q_im = (coef_ref[pl.ds(i * half, half), :] for i in range(3))
    top, bot = spec[:half], spec[half:]
    return jnp.concatenate([top * p_re - bot * q_im, top * q_im + bot * s_re], axis=0)


def _conv_silu_chunk(refs, c, nc, w, b):
    seq = refs[0].shape[0]
    r0 = _chunk_start(c)
    if isinstance(c, int):
        p0, n0 = max(r0 - HALO, 0), min(r0 + CHUNK, seq - HALO)
    else:
        p0 = pl.multiple_of(jnp.maximum(r0 - HALO, 0), HALO)
        n0 = pl.multiple_of(jnp.minimum(r0 + CHUNK, seq - HALO), HALO)

    def rows(start, size):
        return jnp.concatenate([ref[pl.ds(start, size), :] for ref in refs], axis=1).astype(F32)

    prev, main, nxt = rows(p0, HALO), rows(r0, CHUNK), rows(n0, HALO)
    prev = jnp.where(c > 0, prev, 0.0)
    nxt = jnp.where(c < nc - 1, nxt, 0.0)
    ext = jnp.concatenate([prev, main, nxt], axis=0)
    total = CHUNK + 2 * HALO
    xm2 = pltpu.roll(ext, 2, 0)[HALO:HALO + CHUNK]
    xm1 = pltpu.roll(ext, 1, 0)[HALO:HALO + CHUNK]
    xp1 = pltpu.roll(ext, total - 1, 0)[HALO:HALO + CHUNK]
    y = w[0:1] * xm2 + w[1:2] * xm1 + w[2:3] * main + w[3:4] * xp1 + b
    return _silu(y)


def _chunk_operands(lhsp_ref, rowpk_ref, c):
    r0 = _chunk_start(c)
    return lhsp_ref[pl.ds(r0, CHUNK), :], rowpk_ref[0, :, pl.ds(r0, CHUNK)]


def _head_decay(lt, expand_ref, direction=None):
    if direction is None:
        ex = expand_ref[...]
    else:
        ex = expand_ref[:, direction * GROUP_W:(direction + 1) * GROUP_W]
    return jnp.exp2(jnp.dot(lt, ex, preferred_element_type=F32))


def _xs_blockdiag(xs_bf):
    lane = lax.broadcasted_iota(jnp.int32, xs_bf.shape, 1)
    zero = jnp.zeros_like(xs_bf)
    return jnp.concatenate(
        [jnp.where((lane >= r * HEAD_DIM) & (lane < (r + 1) * HEAD_DIM), xs_bf, zero)
         for r in range(HPG)], axis=0)


def _bt_weighted(bt_bf, rp, direction):
    base = ROW_WEND + direction * HPG
    w = rp[base:base + HPG, :].astype(BF16)
    return jnp.concatenate([bt_bf * w[r:r + 1, :] for r in range(HPG)], axis=1)


def _ctx_kernel(slab_ref, lhsp_ref, rowpk_ref, cwx_ref, cbx_ref, cwb_ref, cbb_ref,
                expand_ref, o_ref):
    xs_ref = slab_ref.at[:, 0:SLAB_B]
    b_ref = slab_ref.at[:, SLAB_B:SLAB_C]
    nc = xs_ref.shape[0] // CHUNK
    cw = jnp.concatenate([cwx_ref[...], cwb_ref[...]], axis=1)
    cb = jnp.concatenate([cbx_ref[...], cbb_ref[...]], axis=1)
    contrib, decay = [], []
    for c in range(nc):
        xb = _conv_silu_chunk((xs_ref, b_ref), c, nc, cw, cb)
        xs_bf = xb[:, :GROUP_W].astype(BF16)
        bt = xb[:, GROUP_W:].T.astype(BF16)
        lt, rp = _chunk_operands(lhsp_ref, rowpk_ref, c)
        lhs = jnp.concatenate([_bt_weighted(bt, rp, 0), _bt_weighted(bt, rp, 1)], axis=0)
        contrib.append(jnp.dot(lhs, _xs_blockdiag(xs_bf), preferred_element_type=F32))
        decay.append(_head_decay(lt, expand_ref))
    hf = jnp.zeros((D_STATE, GROUP_W), F32)
    for c in range(nc):
        hf = hf * decay[c][CHUNK - 1:CHUNK, :GROUP_W] + contrib[c][:D_STATE]
    hb = jnp.zeros((D_STATE, GROUP_W), F32)
    for c in reversed(range(nc)):
        hb = hb * decay[c][0:1, GROUP_W:] + contrib[c][D_STATE:]
    o_ref[0, 0, 0] = hf
    o_ref[0, 0, 1] = hb


def _ctx_states(proj_ctx, lhsp, rowpk, conv_w, conv_b, expand, batch, seq):
    bcol = X_END // D_STATE
    const = lambda b, g: (0, 0)
    return pl.pallas_call(
        _ctx_kernel,
        grid=(batch, N_GROUPS),
        in_specs=[pl.BlockSpec((seq, SLAB_W), lambda b, g: (b, g)),
                  pl.BlockSpec((seq, CHUNK), lambda b, g: (b, g)),
                  pl.BlockSpec((1, ROWS_PER_GROUP, seq), lambda b, g: (b, g, 0)),
                  pl.BlockSpec((SSM_CONV, GROUP_W), lambda b, g: (0, g)),
                  pl.BlockSpec((1, GROUP_W), lambda b, g: (0, g)),
                  pl.BlockSpec((SSM_CONV, D_STATE), lambda b, g: (0, bcol + g)),
                  pl.BlockSpec((1, D_STATE), lambda b, g: (0, bcol + g)),
                  pl.BlockSpec(expand.shape, const)],
        out_specs=pl.BlockSpec((1, 1, 2, D_STATE, GROUP_W), lambda b, g: (b, g, 0, 0, 0)),
        out_shape=jax.ShapeDtypeStruct((batch, N_GROUPS, 2, D_STATE, GROUP_W), F32),
        compiler_params=pltpu.CompilerParams(dimension_semantics=("parallel", "parallel")),
        name="ctx_states",
    )(proj_ctx, lhsp, rowpk, conv_w, conv_b, conv_w, conv_b, expand)


def _ssd_kernel(slab_ref, lhsp_ref, rowpk_ref, cwx_ref, cbx_ref, cwb_ref,
                cbb_ref, cwc_ref, cbc_ref, dsk_ref, nw_ref, h0_ref, rest_ref, expand_ref, eye_ref,
                o_ref, cx_s, cbt_s, cc_s, stash_s, hf_s, hb_s):
    xs_ref = slab_ref.at[:, 0:SLAB_B]
    b_ref = slab_ref.at[:, SLAB_B:SLAB_C]
    c_ref = slab_ref.at[:, SLAB_C:SLAB_Z]
    z_ref = slab_ref.at[:, SLAB_Z:SLAB_W]
    seq = xs_ref.shape[0]
    nc = seq // CHUNK
    cw = jnp.concatenate([cwx_ref[...], cwb_ref[...], cwc_ref[...]], axis=1)
    cb =jnp.concatenate([cbx_ref[...], cbb_ref[...], cbc_ref[...]], axis=1)
    hf_s[...] = h0_ref[0, 0, 0]
    hb_s[...] = h0_ref[0, 0, 1]

    def back_body(i, carry):
        c = nc - 1 - i
        r0 = _chunk_start(c)
        xbc = _conv_silu_chunk((xs_ref, b_ref, c_ref), c, nc, cw, cb)
        xs_bf = xbc[:, :GROUP_W].astype(BF16)
        bt = xbc[:, GROUP_W:GROUP_W + D_STATE].T.astype(BF16)
        cx_s[pl.ds(r0, CHUNK), :] = xs_bf
        cbt_s[pl.ds(r0, CHUNK), :] = bt
        cc_s[pl.ds(r0, CHUNK), :] = xbc[:, GROUP_W + D_STATE:].astype(BF16)
        lt, rp = _chunk_operands(lhsp_ref, rowpk_ref, c)
        hb = hb_s[...]
        stash_s[pl.ds(r0, CHUNK), :] = hb.astype(BF16)
        dec = _head_decay(lt, expand_ref, 1)[0:1, :]
        contrib = jnp.dot(_bt_weighted(bt, rp, 1), _xs_blockdiag(xs_bf), preferred_element_type=F32)
        hb_s[...] = hb * dec + contrib
        return carry

    lax.fori_loop(0, nc, back_body, 0, unroll=16)

    prow = lax.broadcasted_iota(jnp.int32, (ROWS_PER_GROUP, HD * CHUNK), 0)
    pcol = lax.broadcasted_iota(jnp.int32, (ROWS_PER_GROUP, HD * CHUNK), 1)
    own_block = (prow < ROW_WEND) & (prow % HD == pcol // CHUNK)
    dsk, nw = dsk_ref[...], nw_ref[...]

    def fwd_body(c, carry):
        r0 = _chunk_start(c)
        xs_bf = cx_s[pl.ds(r0, CHUNK), :]
        bt = cbt_s[pl.ds(r0, CHUNK), :]
        cc = cc_s[pl.ds(r0, CHUNK), :]
        lt, rp = _chunk_operands(lhsp_ref, rowpk_ref, c)
        scores = jnp.dot(cc, bt, preferred_element_type=F32)
        rows = jnp.concatenate([rp.astype(BF16)] * HD, axis=1)
        rows = jnp.where(own_block, rows, jnp.zeros_like(rows))
        seg = jnp.dot(jnp.concatenate([lt, eye_ref[...]], axis=1),
                      jnp.concatenate([rows, rest_ref[...]], axis=0), preferred_element_type=F32)
        decay_dt = jnp.exp2(seg)
        mix = jnp.concatenate(
            [(scores * (decay_dt[:, r * CHUNK:(r + 1) * CHUNK]
                        + decay_dt[:, (HPG + r) * CHUNK:(HPG + r + 1) * CHUNK])).astype(BF16)
             for r in range(HPG)], axis=1)
        lhs = jnp.concatenate([mix, _bt_weighted(bt, rp, 0)], axis=0)
        prod = jnp.dot(lhs, _xs_blockdiag(xs_bf), preferred_element_type=F32)
        hf = hf_s[...]
        e = _head_decay(lt, expand_ref)
        states = jnp.concatenate([hf.astype(BF16), stash_s[pl.ds(r0, CHUNK), :]], axis=1)
        carried = jnp.dot(cc, states, preferred_element_type=F32) * e
        y = dsk * xs_bf.astype(F32) + prod[:CHUNK] + carried[:, :GROUP_W] + carried[:, GROUP_W:]
        hf_s[...] = hf * e[CHUNK - 1:CHUNK, :GROUP_W] + prod[CHUNK:]
        y = y * z_ref[pl.ds(r0, CHUNK), :].astype(F32)
        ms = jnp.mean(y * y, axis=-1, keepdims=True)
        o_ref[pl.ds(r0, CHUNK), :] = (y * lax.rsqrt(ms + EPS) * nw).astype(o_ref.dtype)
        return carry

    lax.fori_loop(0, nc, fwd_body, 0, unroll=32)


def _ssd(proj, lhsp, rowpk, conv_w, conv_b, dskip_e, norm_w, h0, rest, expand, eye, batch, seq):
    bcol = X_END // D_STATE
    ccol = B_END // D_STATE
    gmap = lambda b, g: (0, g)
    const = lambda b, g: (0, 0)
    return pl.pallas_call(
        _ssd_kernel,
        grid=(batch, N_GROUPS),
        in_specs=[pl.BlockSpec((seq, SLAB_W), lambda b, g: (b, g)),
                  pl.BlockSpec((seq, CHUNK), lambda b, g: (b, g)),
                  pl.BlockSpec((1, ROWS_PER_GROUP, seq), lambda b, g: (b, g, 0)),
                  pl.BlockSpec((SSM_CONV, GROUP_W), gmap),
                  pl.BlockSpec((1, GROUP_W), gmap),
                  pl.BlockSpec((SSM_CONV, D_STATE), lambda b, g: (0, bcol + g)),
                  pl.BlockSpec((1, D_STATE), lambda b, g: (0, bcol + g)),
                  pl.BlockSpec((SSM_CONV, D_STATE), lambda b, g: (0, ccol + g)),
                  pl.BlockSpec((1, D_STATE), lambda b, g: (0, ccol + g)),
                  pl.BlockSpec((1, GROUP_W), gmap),
                  pl.BlockSpec((1, GROUP_W), gmap),
                  pl.BlockSpec((1, 1, 2, D_STATE, GROUP_W), lambda b, g: (b, g, 0, 0, 0)),
                  pl.BlockSpec(rest.shape, const),
                  pl.BlockSpec(expand.shape, const),
                  pl.BlockSpec(eye.shape, const)],
        out_specs=pl.BlockSpec((seq, GROUP_W), lambda b, g: (b, g)),
        out_shape=jax.ShapeDtypeStruct((batch * seq, D_INNER), BF16),
        scratch_shapes=[pltpu.VMEM((seq, GROUP_W), BF16),
                        pltpu.VMEM((seq, D_STATE), BF16),
                        pltpu.VMEM((seq, D_STATE), BF16),
                        pltpu.VMEM((seq, GROUP_W), BF16),
                        pltpu.VMEM((D_STATE, GROUP_W), F32),
                        pltpu.VMEM((D_STATE, GROUP_W), F32)],
        compiler_params=pltpu.CompilerParams(
            dimension_semantics=("parallel", "parallel"), vmem_limit_bytes=VMEM_LIMIT_BYTES),
        name="ssd",
    )(proj, lhsp, rowpk, conv_w, conv_b, conv_w, conv_b, conv_w, conv_b,
      dskip_e, norm_w, h0, rest, expand, eye)


N_TAIL_PROJ = 5
TAIL_TM = 512


DFT_N = 2 * GRID_W
DFT_HALF = DFT_N // 2
CONF_TAPS_PADDED = 32


def _conv_dft_constants():
    lags = [CONF_KERNEL // 2 - k for k in range(CONF_KERNEL)]
    return _dft_conv_matrices(DFT_N, GRID_W, GRID_W, GRID_W, lags, CONF_TAPS_PADDED)


def _tail_kernel(yn_ref, x_ref, xnext_ref, mod_ref, modnext_ref, nw_ref, wt0_ref, wt1_ref, wt2_ref,
                 wt3_ref, wt4_ref, cw_ref, cb_ref, fwd_ref, inv_ref, taps_ref, lnw_ref, lnb_ref,
                 wos_ref, woc_ref, wo_ref, fnw_ref, o_ref, ub_s, uc_s, proj_s, act_s, coef_s):
    tm = x_ref.shape[0]
    nseq = tm // GRID_W
    wt_refs = (wt0_ref, wt1_ref, wt2_ref, wt3_ref, wt4_ref)

    def project(xv, mref):
        h = _modulated_norm(xv, nw_ref[...], mref).astype(BF16)
        for t in range(N_TAIL_PROJ):
            proj_s[t] = jnp.dot(h, wt_refs[t][...], preferred_element_type=F32)

    @pl.when(pl.program_id(0) == 0)
    def _():
        project(x_ref[...], mod_ref)
        coef_s[...] = jnp.dot(taps_ref[...], cw_ref[...], preferred_element_type=F32,
                              precision=lax.Precision.HIGHEST)

    ub_s[...] = (proj_s[0] * _sigmoid(proj_s[1])).astype(BF16)
    act_s[0] = _silu(proj_s[2])
    act_s[1] = _sigmoid(proj_s[3])
    act_s[2] = _sigmoid(proj_s[4])
    project(xnext_ref[...], modnext_ref)

    for q in range(nseq):
        rows = pl.ds(q * GRID_W, GRID_W)
        spec = jnp.dot(fwd_ref[...], ub_s[rows, :], preferred_element_type=F32)
        prod = _spectrum_product(spec, coef_s).astype(BF16)
        uc_s[rows, :] = jnp.dot(inv_ref[...], prod, preferred_element_type=F32) + cb_ref[...]
    x = x_ref[...]
    uc = uc_s[...]
    mu = jnp.mean(uc, axis=-1, keepdims=True)
    dev = uc - mu
    var = jnp.mean(dev * dev, axis=-1, keepdims=True)
    ln = dev * lax.rsqrt(var + EPS) * lnw_ref[...] + lnb_ref[...]
    u2 = _silu(ln) * act_s[0]
    branch_conf = jnp.dot(u2.astype(BF16), woc_ref[...], preferred_element_type=F32)
    branch_ssm = jnp.dot(yn_ref[...], wos_ref[...], preferred_element_type=F32)
    merged = act_s[1] * branch_ssm + act_s[2] * branch_conf
    out = jnp.dot(merged.astype(BF16), wo_ref[...], preferred_element_type=F32)
    xn = x + mod_ref[0, 2:3, :] * out
    ms = jnp.mean(xn * xn, axis=-1, keepdims=True)
    o_ref[...] = xn * lax.rsqrt(ms + EPS) * fnw_ref[...]


def _tail(yn, x2d, mod3, norm_w, w_all, conf_w, conf_b, ln_w, ln_b, w_os, w_oc, w_o, fn_w, *,
          tm, seq):
    m, d = x2d.shape
    tiles_per_batch = seq // tm
    const = lambda i: (0, 0)
    fwd, inv, taps = _conv_dft_constants()
    n_tiles = m // tm
    nxt = lambda i: jnp.minimum(i + 1, n_tiles - 1)
    once = dict(pipeline_mode=pl.Buffered(1))
    return pl.pallas_call(
        _tail_kernel,
        grid=(m // tm,),
        in_specs=[pl.BlockSpec((tm, D_INNER), lambda i: (i, 0)),
                  pl.BlockSpec((tm, d), lambda i: (i, 0)),
                  pl.BlockSpec((tm, d), lambda i: (nxt(i), 0)),
                  pl.BlockSpec((1, 3, d), lambda i: (i // tiles_per_batch, 0, 0)),
                  pl.BlockSpec((1, 3, d), lambda i: (nxt(i) // tiles_per_batch, 0, 0)),
                  pl.BlockSpec((1, d), const),
                  *[pl.BlockSpec((d, D_CONF), functools.partial(lambda t, i: (0, t), P_COLS // D_CONF + t),
                                 **once) for t in range(N_TAIL_PROJ)],
                  pl.BlockSpec((CONF_TAPS_PADDED, D_CONF), const),
                  pl.BlockSpec((1, D_CONF), const),
                  pl.BlockSpec(fwd.shape, const),
                  pl.BlockSpec(inv.shape, const),
                  pl.BlockSpec(taps.shape, const),
                  pl.BlockSpec((1, D_CONF), const),
                  pl.BlockSpec((1, D_CONF), const),
                  pl.BlockSpec((D_INNER, D_MODEL), const, **once),
                  pl.BlockSpec((D_CONF, D_MODEL), const, **once),
                  pl.BlockSpec((D_MODEL, D_MODEL), const, **once),
                  pl.BlockSpec((1, D_MODEL), const)],
        out_specs=pl.BlockSpec((tm, d), lambda i: (i, 0)),
        out_shape=jax.ShapeDtypeStruct((m, d), F32),
        scratch_shapes=[pltpu.VMEM((tm, D_CONF), BF16),
                        pltpu.VMEM((tm, D_CONF), F32),
                        pltpu.VMEM((N_TAIL_PROJ, tm, D_CONF), F32),
                        pltpu.VMEM((3, tm, D_CONF), F32),
                        pltpu.VMEM((3 * DFT_HALF, D_CONF), F32)],
        compiler_params=pltpu.CompilerParams(
            dimension_semantics=("arbitrary",), vmem_limit_bytes=VMEM_LIMIT_BYTES),
        name="tail",
    )(yn, x2d, x2d, mod3, mod3, norm_w, *([w_all] * N_TAIL_PROJ), conf_w, conf_b, fwd.astype(BF16),
      inv.astype(BF16), taps, ln_w, ln_b, w_os, w_oc, w_o, fn_w)


def _head_perm():
    return np.array([d * N_HEADS + g * HPG + r
                     for g in range(N_GROUPS) for d in range(2) for r in range(HPG)])


def kernel(x, c, ctx, c_ctx, w_mod, b_mod, norm_w, w_in, ssm_conv_w, ssm_conv_b, dt_bias, a_log,
           d_skip, ssm_norm_w, w_out_ssm, conf_conv_w, conf_conv_b, conf_ln_w, conf_ln_b,
           w_out_conf, w_out, final_norm_w):
    batch, seq, d = x.shape
    ctx_len = ctx.shape[1]
    assert w_in.shape[0] == 1, "single trunk layer"
    assert d == D_MODEL and seq % CHUNK == 0 and ctx_len % CHUNK == 0 and batch + 1 <= 8

    w_in0 = w_in[0]
    slabs = []
    for g in range(N_GROUPS):
        slabs += [w_in0[:, g * GROUP_W:(g + 1) * GROUP_W],
                  w_in0[:, X_END + g * D_STATE:X_END + (g + 1) * D_STATE],
                  w_in0[:, B_END + g * D_STATE:B_END + (g + 1) * D_STATE],
                  w_in0[:, DT_END + g * GROUP_W:DT_END + (g + 1) * GROUP_W]]
    w_all = jnp.concatenate(slabs + [w_in0[:, Z_END:]], axis=1).astype(BF16)
    perm = _head_perm()
    pad_heads = DT_LANES - 2 * N_HEADS
    w_dt = jnp.pad(w_in0[:, C_END:DT_END][:, perm], ((0, 0), (0, pad_heads))).astype(BF16)
    bias_p = jnp.pad(dt_bias[0].reshape(-1)[perm], (0, pad_heads)).reshape(1, DT_LANES)
    alog_p = jnp.pad(a_log[0].reshape(-1)[perm], (0, pad_heads)).reshape(1, DT_LANES)
    conv_w = ssm_conv_w[0]
    conv_b = ssm_conv_b[0].reshape(1, C_END)
    dskip_e = jnp.repeat(d_skip[0], HEAD_DIM).reshape(1, D_INNER)
    nw2 = norm_w[0].reshape(1, d)
    rest, expand, eye, scatter = _decay_constants()

    c_all = jnp.concatenate([c, c_ctx[None, :], jnp.zeros((8 - batch - 1, d), F32)], axis=0)
    mod3 = _mod(c_all, w_mod[0], b_mod[0]).reshape(8, 3, d)

    ctx_rows = batch * ctx_len
    tm_ctx = min(IN_PROJ_TM, ctx_rows)
    proj_ctx, dtraw_ctx = _in_proj(ctx.reshape(ctx_rows, d), mod3, nw2, w_all, w_dt,
                                   tm=tm_ctx, n_cols=P_COLS, plain_cols=C_END,
                                   mod_row=lambda i: batch)
    lhsp_ctx, rowpk_ctx = _dt_prep(dtraw_ctx, bias_p, alog_p, scatter, batch, ctx_len)
    h0 = _ctx_states(proj_ctx, lhsp_ctx, rowpk_ctx, conv_w, conv_b, expand, batch, ctx_len)

    m = batch * seq
    x2d = x.reshape(m, d)
    tm = min(IN_PROJ_TM, seq)
    tiles_per_batch = seq // tm
    proj, dtraw = _in_proj(x2d, mod3, nw2, w_all, w_dt, tm=tm, n_cols=P_COLS, plain_cols=C_END,
                           mod_row=lambda i: i // tiles_per_batch)
    lhsp, rowpk = _dt_prep(dtraw, bias_p, alog_p, scatter, batch, seq)
    yn = _ssd(proj, lhsp, rowpk, conv_w, conv_b, dskip_e, ssm_norm_w[0].reshape(1, D_INNER), h0,
              rest, expand, eye, batch, seq)
    conf_w = jnp.pad(conf_conv_w[0], ((0, CONF_TAPS_PADDED - CONF_KERNEL), (0, 0)))
    out = _tail(yn, x2d, mod3, nw2, w_all, conf_w, conf_conv_b[0].reshape(1, D_CONF),
                conf_ln_w[0].reshape(1, D_CONF), conf_ln_b[0].reshape(1, D_CONF),
                w_out_ssm[0].astype(BF16), w_out_conf[0].astype(BF16), w_out[0].astype(BF16),
                final_norm_w.reshape(1, d), tm=min(TAIL_TM, seq), seq=seq)
    return out.reshape(batch, seq, d)
```
